```python
import jax, jax.numpy as jnp
from jax import lax
import numpy as np

D_MODEL = 1024
BATCH = 8
SEQ = 2048
DEPTH = 1
DEC_BATCH = 128
DEC_SEQ = 1
PAST_LEN = 16384
PAGE_SIZE = 128

D_A = D_MODEL
G_A = 8
D_B = D_MODEL
G_B = 8
DG_B = D_B // G_B
CHUNK = 128
CONV_W = 3
D_FF = 2816
P_DIM = 256
N_IN = 3 * D_A + 2 * D_B + 2 * D_MODEL
EPS = 1e-6

kernel_name = "hybrid_shortconv_gmlp_convffn_step"


def rmsnorm(x, g):
    xf = x.astype(jnp.float32)
    xf = xf * lax.rsqrt(jnp.mean(xf * xf, axis=-1, keepdims=True) + EPS)
    return (xf * g.astype(jnp.float32)).astype(x.dtype)


def causal_dwconv(x, w, prev):
    L = x.shape[1]
    xp = jnp.concatenate([prev, x], axis=1)
    y = xp[:, 0:L] * w[0]
    for k in range(1, CONV_W):
        y = y + xp[:, k:k + L] * w[k]
    return y, xp[:, -(CONV_W - 1):]


def chunk_spatial(v, w_s, b_s):
    B, L, _ = v.shape
    lc = min(L, CHUNK)
    n_c = -(-L // lc)
    lp = n_c * lc
    vp = jnp.pad(v, ((0, 0), (0, lp - L), (0, 0))).reshape(B, n_c, lc, G_B, DG_B)
    w = jnp.tril(w_s[:, :lc, :lc])
    out = jnp.einsum("gts,bcsgd->bctgd", w, vp) + b_s[:, :lc].T[None, None, :, :, None]
    return out.reshape(B, lp, D_B)[:, :L]


def layer(h, p, prev_a, prev_ffn, g_mix, w_in, w_conv_a, w_out_a, g_v, w_spatial, b_spatial,
          w_out_b, w_o, g_ffn, w_up, w_conv_ffn, w_down, g_ple, w_ple_gate, w_ple):
    L = h.shape[1]
    xn = rmsnorm(h, g_mix)
    z = xn @ w_in
    x_a, b_a, c_a, uv, ga, gb = jnp.split(
        z, [D_A, 2 * D_A, 3 * D_A, 3 * D_A + 2 * D_B, 3 * D_A + 2 * D_B + D_MODEL], axis=-1)
    a_conv, new_a = causal_dwconv(c_a * x_a, w_conv_a, prev_a)
    y_a = (b_a * a_conv) @ w_out_a
    uv = jax.nn.gelu(uv)
    u, v = jnp.split(uv, 2, axis=-1)
    v = rmsnorm(v, g_v)
    s = chunk_spatial(v, w_spatial, b_spatial)
    y_b = (u * s) @ w_out_b
    start = ((L - 1) // CHUNK) * CHUNK
    chunk_v = v[:, start:]
    m = jax.nn.sigmoid(ga) * y_a + jax.nn.sigmoid(gb) * y_b
    h = h + m @ w_o
    up = rmsnorm(h, g_ffn) @ w_up
    up_c, new_ffn = causal_dwconv(up, w_conv_ffn, prev_ffn)
    fa, fb = jnp.split(up_c, 2, axis=-1)
    h = h + (jax.nn.gelu(fa) * fb) @ w_down
    gate = jax.nn.sigmoid(rmsnorm(h, g_ple) @ w_ple_gate)
    h = h + gate * (p @ w_ple)
    return h, new_a, chunk_v, new_ffn


def setup_inputs(seed: int = 0) -> dict:
    key = jax.random.key(seed)
    ks = jax.random.split(key, 24)
    f32 = jnp.float32

    def nrm(k, shape, scale):
        return jax.random.normal(k, shape, f32) * scale

    def gain(k, shape):
        return 1.0 + 0.05 * jax.random.normal(k, shape, f32)

    return {
        "x_prompt": nrm(ks[0], (BATCH, SEQ, D_MODEL), 1.0),
        "x_sample": nrm(ks[1], (DEC_BATCH, DEC_SEQ, D_MODEL), 1.0),
        "p_prompt": nrm(ks[2], (DEPTH, BATCH, SEQ, P_DIM), 1.0),
        "p_sample": nrm(ks[3], (DEPTH, DEC_BATCH, DEC_SEQ, P_DIM), 1.0),
        "state_conv_a": nrm(ks[4], (DEPTH, DEC_BATCH, CONV_W - 1, D_A), 1.0),
        "state_conv_ffn": nrm(ks[5], (DEPTH, DEC_BATCH, CONV_W - 1, 2 * D_FF), 1.0),
        "g_mix": gain(ks[6], (DEPTH, D_MODEL)),
        "w_in": nrm(ks[7], (DEPTH, D_MODEL, N_IN), D_MODEL ** -0.5),
        "w_conv_a": nrm(ks[8], (DEPTH, CONV_W, D_A), CONV_W ** -0.5),
        "w_out_a": nrm(ks[9], (DEPTH, D_A, D_MODEL), D_A ** -0.5),
        "g_v": gain(ks[10], (DEPTH, D_B)),
        "w_spatial": nrm(ks[11], (DEPTH, G_B, CHUNK, CHUNK), CHUNK ** -0.5),
        "b_spatial": 1.0 + 0.05 * jax.random.normal(ks[12], (DEPTH, G_B, CHUNK), f32),
        "w_out_b": nrm(ks[13], (DEPTH, D_B, D_MODEL), D_B ** -0.5),
        "w_o": nrm(ks[14], (DEPTH, D_MODEL, D_MODEL), 0.5 * D_MODEL ** -0.5),
        "g_ffn": gain(ks[15], (DEPTH, D_MODEL)),
        "w_up": nrm(ks[16], (DEPTH, D_MODEL, 2 * D_FF), D_MODEL ** -0.5),
        "w_conv_ffn": nrm(ks[17], (DEPTH, CONV_W, 2 * D_FF), CONV_W ** -0.5),
        "w_down": nrm(ks[18], (DEPTH, D_FF, D_MODEL), 0.5 * D_FF ** -0.5),
        "g_ple": gain(ks[19], (DEPTH, D_MODEL)),
        "w_ple_gate": nrm(ks[20], (DEPTH, D_MODEL, D_MODEL), D_MODEL ** -0.5),
        "w_ple": nrm(ks[21], (DEPTH, P_DIM, D_MODEL), P_DIM ** -0.5),
        "g_final": gain(ks[22], (D_MODEL,)),
    }


def reference(x_prompt, x_sample, p_prompt, p_sample, state_conv_a, state_conv_ffn,
              g_mix, w_in, w_conv_a, w_out_a, g_v, w_spatial, b_spatial, w_out_b, w_o,
              g_ffn, w_up, w_conv_ffn, w_down, g_ple, w_ple_gate, w_ple, g_final):
    hp, hs = x_prompt, x_sample
    ca_p, ca_s, cv_p, cv_s, cf_p, cf_s = [], [], [], [], [], []
    for i in range(DEPTH):
        params = (g_mix[i], w_in[i], w_conv_a[i], w_out_a[i], g_v[i], w_spatial[i], b_spatial[i],
                  w_out_b[i], w_o[i], g_ffn[i], w_up[i], w_conv_ffn[i], w_down[i], g_ple[i],
                  w_ple_gate[i], w_ple[i])
        zero_a = jnp.zeros((hp.shape[0], CONV_W - 1, D_A), hp.dtype)
        zero_f = jnp.zeros((hp.shape[0], CONV_W - 1, 2 * D_FF), hp.dtype)
        hp, na, vp, nf = layer(hp, p_prompt[i], zero_a, zero_f, *params)
        ca_p.append(na); cv_p.append(vp); cf_p.append(nf)
        hs, na, vs, nf = layer(hs, p_sample[i], state_conv_a[i], state_conv_ffn[i], *params)
        ca_s.append(na); cv_s.append(vs); cf_s.append(nf)
    y_prompt = rmsnorm(hp, g_final)
    y_sample = rmsnorm(hs, g_final)
    new_conv_a_prompt = jnp.stack(ca_p)
    new_conv_a_sample = jnp.stack(ca_s)
    chunk_v_prompt = jnp.stack(cv_p)
    chunk_v_sample = jnp.stack(cv_s)
    new_conv_ffn_prompt = jnp.stack(cf_p)
    new_conv_ffn_sample = jnp.stack(cf_s)
    return (y_prompt, y_sample, new_conv_a_prompt, new_conv_a_sample, chunk_v_prompt,
            chunk_v_sample, new_conv_ffn_prompt, new_conv_ffn_sample)
```

```python
import functools

import jax
import jax.numpy as jnp
from jax import lax
from jax.experimental import pallas as pl
from jax.experimental.pallas import tpu as pltpu

D_MODEL = 1024
D_A = 1024
D_B = 1024
G_B = 8
DG_B = D_B // G_B
CHUNK = 128
CONV_W = 3
D_FF = 2816
P_DIM = 256
EPS = 1e-6

SUBLANES = 8
TM = 256
FF_CHUNK = 1408
VMEM_LIMIT_BYTES = 62 * 1024 * 1024

_XA, _BA, _CA, _U, _V, _GA, _GB = (0, D_A, 2 * D_A, 3 * D_A, 3 * D_A + D_B,
                                   3 * D_A + 2 * D_B, 3 * D_A + 2 * D_B + D_MODEL)

bf16 = jnp.bfloat16
f32 = jnp.float32


def _rms(x, g):
    ms = jnp.mean(x * x, axis=-1, keepdims=True)
    return x * lax.rsqrt(ms + EPS) * g


def _dot(a, b):
    return jnp.dot(a, b, preferred_element_type=f32)


def _conv_seq(x, w, p0, p1):
    t = x.shape[0]
    r1 = pltpu.roll(x, 1, axis=0)
    r2 = pltpu.roll(x, 2, axis=0)
    w0, w1, w2 = w[0:1], w[1:2], w[2:3]
    y = w0 * r2 + w1 * r1 + w2 * x
    row = lax.broadcasted_iota(jnp.int32, (SUBLANES, 1), 0)
    xh = x[0:SUBLANES]
    h1 = jnp.where(row == 0, p1, r1[0:SUBLANES])
    h2 = jnp.where(row == 0, p0, jnp.where(row == 1, p1, r2[0:SUBLANES]))
    yh = w0 * h2 + w1 * h1 + w2 * xh
    return jnp.concatenate([yh, y[SUBLANES:t]], axis=0)


def _layer_body(seq, x, p, prev_a, prev_f, refs):
    (g_mix, w_in, w_conv_a, w_out_a, g_v, w_sp, b_sp, w_out_b, w_o, g_ffn, w_up, w_conv_ffn,
     w_down, g_ple, w_ple_gate, w_ple, g_final) = refs
    t = x.shape[0]
    xn = _rms(x, g_mix[...]).astype(bf16)

    xa = _dot(xn, w_in[:, _XA:_XA + D_A])
    ca = _dot(xn, w_in[:, _CA:_CA + D_A])
    cx = ca * xa
    wca = w_conv_a[...]
    if seq:
        conv_a = _conv_seq(cx, wca, prev_a[6:7], prev_a[7:8])
    else:
        conv_a = wca[0:1] * prev_a[0] + wca[1:2] * prev_a[1] + wca[2:3] * cx
    ba = _dot(xn, w_in[:, _BA:_BA + D_A])
    y_a = _dot((ba * conv_a).astype(bf16), w_out_a[...])

    u = jax.nn.gelu(_dot(xn, w_in[:, _U:_U + D_B]))
    v = jax.nn.gelu(_dot(xn, w_in[:, _V:_V + D_B]))
    v = _rms(v, g_v[...])
    if seq:
        vb = v.astype(bf16)
        ti = lax.broadcasted_iota(jnp.int32, (CHUNK, CHUNK), 0)
        si = lax.broadcasted_iota(jnp.int32, (CHUNK, CHUNK), 1)
        causal = si <= ti
        bias = b_sp[...]
        cols = []
        for g in range(G_B):
            wg = jnp.where(causal, w_sp[g], jnp.zeros((), bf16))
            bg = bias[:, g:g + 1]
            rows = [_dot(wg, vb[c * CHUNK:(c + 1) * CHUNK, g * DG_B:(g + 1) * DG_B]) + bg
                    for c in range(t // CHUNK)]
            cols.append(jnp.concatenate(rows, axis=0))
        s = jnp.concatenate(cols, axis=1)
    else:
        s = w_sp[...] * v + b_sp[...]
    y_b = _dot((u * s).astype(bf16), w_out_b[...])

    ga = _dot(xn, w_in[:, _GA:_GA + D_MODEL])
    gb = _dot(xn, w_in[:, _GB:_GB + D_MODEL])
    m = jax.nn.sigmoid(ga) * y_a + jax.nn.sigmoid(gb) * y_b
    h = x + _dot(m.astype(bf16), w_o[...])

    xn2 = _rms(h, g_ffn[...]).astype(bf16)
    acc = None
    up_chunks = []
    for c0 in range(0, D_FF, FF_CHUNK):
        halves = []
        for off in (c0, D_FF + c0):
            up = _dot(xn2, w_up[:, off:off + FF_CHUNK])
            wcf = w_conv_ffn[:, off:off + FF_CHUNK]
            if seq:
                conv = _conv_seq(up, wcf, prev_f[6:7, off:off + FF_CHUNK],
                                 prev_f[7:8, off:off + FF_CHUNK])
            else:
                conv = (wcf[0:1] * prev_f[0][:, off:off + FF_CHUNK]
                        + wcf[1:2] * prev_f[1][:, off:off + FF_CHUNK] + wcf[2:3] * up)
            up_chunks.append((off, up))
            halves.append(conv)
        hid = (jax.nn.gelu(halves[0]) * halves[1]).astype(bf16)
        part = _dot(hid, w_down[c0:c0 + FF_CHUNK, :])
        acc = part if acc is None else acc + part
    h = h + acc

    gate = jax.nn.sigmoid(_dot(_rms(h, g_ple[...]).astype(bf16), w_ple_gate[...]))
    h = h + gate * _dot(p.astype(bf16), w_ple[...])
    return _rms(h, g_final[...]), cx, v, up_chunks


def _prompt_kernel(x_ref, p_ref, *rest):
    refs = rest[:17]
    y_ref, ca_ref, cv_ref, cf_ref, carry_a, carry_f = rest[17:]
    l = pl.program_id(1)

    @pl.when(l == 0)
    def _():
        carry_a[...] = jnp.zeros_like(carry_a)
        carry_f[...] = jnp.zeros_like(carry_f)

    y, cx, v, up_chunks = _layer_body(True, x_ref[0], p_ref[0], carry_a[...], carry_f[...], refs)
    y_ref[0] = y
    carry_a[...] = cx[TM - SUBLANES:TM]
    for off, up in up_chunks:
        carry_f[:, off:off + FF_CHUNK] = up[TM - SUBLANES:TM]

    @pl.when(l == pl.num_programs(1) - 1)
    def _():
        ca_ref[0] = cx[TM - (CONV_W - 1):TM]
        cv_ref[0] = v[TM - CHUNK:TM]
        for off, up in up_chunks:
            cf_ref[0, :, off:off + FF_CHUNK] = up[TM - (CONV_W - 1):TM]


def _sample_kernel(x_ref, p_ref, sa_ref, sf_ref, *rest):
    refs = rest[:17]
    y_ref, ca_ref, cv_ref, cf_ref = rest[17:]
    sa0, sa1 = sa_ref[:, 0:D_A], sa_ref[:, D_A:2 * D_A]
    sf0, sf1 = sf_ref[:, 0:2 * D_FF], sf_ref[:, 2 * D_FF:4 * D_FF]
    y, cx, v, up_chunks = _layer_body(False, x_ref[...], p_ref[...], (sa0, sa1), (sf0, sf1), refs)
    y_ref[...] = y
    ca_ref[:, 0:D_A] = sa1
    ca_ref[:, D_A:2 * D_A] = cx
    cv_ref[...] = v
    cf_ref[:, 0:2 * D_FF] = sf1
    for off, up in up_chunks:
        cf_ref[:, 2 * D_FF + off:2 * D_FF + off + FF_CHUNK] = up


def _full(shape):
    return pl.BlockSpec(shape, lambda *_: (0,) * len(shape))


def kernel(x_prompt, x_sample, p_prompt, p_sample, state_conv_a, state_conv_ffn, g_mix, w_in, w_conv_a, w_out_a, g_v, w_spatial, b_spatial, w_out_b, w_o, g_ffn, w_up, w_conv_ffn, w_down, g_ple, w_ple_gate, w_ple, g_final):
    depth = g_mix.shape[0]
    assert depth == 1
    batch, seq_len, _ = x_prompt.shape
    dec_batch, dec_seq, _ = x_sample.shape
    assert dec_seq == 1 and seq_len % TM == 0 and TM % CHUNK == 0

    row = lambda a: a.reshape(1, -1)
    shared = dict(
        g_mix=row(g_mix[0]), w_in=w_in[0].astype(bf16), w_conv_a=w_conv_a[0],
        w_out_a=w_out_a[0].astype(bf16), g_v=row(g_v[0]), w_out_b=w_out_b[0].astype(bf16),
        w_o=w_o[0].astype(bf16), g_ffn=row(g_ffn[0]), w_up=w_up[0].astype(bf16),
        w_conv_ffn=w_conv_ffn[0], w_down=w_down[0].astype(bf16), g_ple=row(g_ple[0]),
        w_ple_gate=w_ple_gate[0].astype(bf16), w_ple=w_ple[0].astype(bf16), g_final=row(g_final))
    order = ("g_mix", "w_in", "w_conv_a", "w_out_a", "g_v", "w_sp", "b_sp", "w_out_b", "w_o",
             "g_ffn", "w_up", "w_conv_ffn", "w_down", "g_ple", "w_ple_gate", "w_ple", "g_final")

    pw = dict(shared, w_sp=w_spatial[0].astype(bf16), b_sp=b_spatial[0].T)
    pw_args = [pw[k] for k in order]
    n_l = seq_len // TM
    y_p, ca_p, cv_p, cf_p = pl.pallas_call(
        _prompt_kernel,
        grid=(batch, n_l),
        in_specs=[pl.BlockSpec((1, TM, D_MODEL), lambda b, l: (b, l, 0)),
                  pl.BlockSpec((1, TM, P_DIM), lambda b, l: (b, l, 0))]
                 + [_full(a.shape) for a in pw_args],
        out_specs=[pl.BlockSpec((1, TM, D_MODEL), lambda b, l: (b, l, 0)),
                   pl.BlockSpec((1, CONV_W - 1, D_A), lambda b, l: (b, 0, 0)),
                   pl.BlockSpec((1, CHUNK, D_B), lambda b, l: (b, 0, 0)),
                   pl.BlockSpec((1, CONV_W - 1, 2 * D_FF), lambda b, l: (b, 0, 0))],
        out_shape=[jax.ShapeDtypeStruct((batch, seq_len, D_MODEL), f32),
                   jax.ShapeDtypeStruct((batch, CONV_W - 1, D_A), f32),
                   jax.ShapeDtypeStruct((batch, CHUNK, D_B), f32),
                   jax.ShapeDtypeStruct((batch, CONV_W - 1, 2 * D_FF), f32)],
        scratch_shapes=[pltpu.VMEM((SUBLANES, D_A), f32), pltpu.VMEM((SUBLANES, 2 * D_FF), f32)],
        compiler_params=pltpu.CompilerParams(
            dimension_semantics=("arbitrary", "arbitrary"), vmem_limit_bytes=VMEM_LIMIT_BYTES),
        name="layer_prompt",
    )(x_prompt, p_prompt[0], *pw_args)

    sw = dict(shared,
              w_sp=row(jnp.repeat(w_spatial[0, :, 0, 0], DG_B)),
              b_sp=row(jnp.repeat(b_spatial[0, :, 0], DG_B)))
    sw_args = [sw[k] for k in order]
    s_in = [x_sample.reshape(dec_batch, D_MODEL), p_sample[0].reshape(dec_batch, P_DIM),
            state_conv_a[0].reshape(dec_batch, (CONV_W - 1) * D_A),
            state_conv_ffn[0].reshape(dec_batch, (CONV_W - 1) * 2 * D_FF)]
    y_s, ca_s, cv_s, cf_s = pl.pallas_call(
        _sample_kernel,
        in_specs=[_full(a.shape) for a in s_in + sw_args],
        out_specs=[_full((dec_batch, D_MODEL)), _full((dec_batch, (CONV_W - 1) * D_A)),
                   _full((dec_batch, D_B)), _full((dec_batch, (CONV_W - 1) * 2 * D_FF))],
        out_shape=[jax.ShapeDtypeStruct((dec_batch, D_MODEL), f32),
                   jax.ShapeDtypeStruct((dec_batch, (CONV_W - 1) * D_A), f32),
                   jax.ShapeDtypeStruct((dec_batch, D_B), f32),
                   jax.ShapeDtypeStruct((dec_batch, (CONV_W - 1) * 2 * D_FF), f32)],
        compiler_params=pltpu.CompilerParams(vmem_limit_bytes=VMEM_LIMIT_BYTES),
        name="layer_sample",
    )(*s_in, *sw_args)

    return (y_p, y_s.reshape(dec_batch, 1, D_MODEL),
            ca_p[None], ca_s.reshape(1, dec_batch, CONV_W - 1, D_A),
            cv_p[None], cv_s.reshape(1, dec_batch, 1, D_B),
            cf_p[None], cf_s.reshape(1, dec_batch, CONV_W - 1, 2 * D_FF))
```

```python
import jax
import jax.numpy as jnp
from jax import lax
from jax.experimental import pallas as pl
from jax.experimental.pallas import tpu as pltpu

D_MODEL = 1024
D_A = 1024
D_B = 1024
G_B = 8
DG_B = D_B // G_B
CHUNK = 128
CONV_W = 3
D_FF = 2816
P_DIM = 256
EPS = 1e-6

SUBLANES = 8
TM = 256
FF_CHUNK = 1408
VMEM_LIMIT_BYTES = 62 * 1024 * 1024

_XA, _BA, _CA, _U, _V, _GA, _GB = (0, D_A, 2 * D_A, 3 * D_A, 3 * D_A + D_B,
                                   3 * D_A + 2 * D_B, 3 * D_A + 2 * D_B + D_MODEL)

bf16 = jnp.bfloat16
f32 = jnp.float32


def _pack_rows(w):
    k, n = w.shape
    wb = w.astype(bf16).reshape(k // 2, 2, n)
    return lax.bitcast_convert_type(jnp.swapaxes(wb, -1, -2), jnp.uint32)


def _rms(x, g):
    ms = jnp.mean(x * x, axis=-1, keepdims=True)
    return x * lax.rsqrt(ms + EPS) * g


def _dot(a, b):
    return jnp.dot(a, b, preferred_element_type=f32)


def _wdot(a, w_ref, r0=0, r1=None, c0=0, c1=None):
    r1 = 2 * w_ref.shape[0] if r1 is None else r1
    c1 = w_ref.shape[1] if c1 is None else c1
    w = pltpu.bitcast(w_ref[r0 // 2:r1 // 2, c0:c1], bf16)
    return _dot(a, w)


def _conv_seq(x, w, p0, p1):
    t = x.shape[0]
    r1 = pltpu.roll(x, 1, axis=0)
    r2 = pltpu.roll(x, 2, axis=0)
    w0, w1, w2 = w[0:1], w[1:2], w[2:3]
    y = w0 * r2 + w1 * r1 + w2 * x
    row = lax.broadcasted_iota(jnp.int32, (SUBLANES, 1), 0)
    xh = x[0:SUBLANES]
    h1 = jnp.where(row == 0, p1, r1[0:SUBLANES])
    h2 = jnp.where(row == 0, p0, jnp.where(row == 1, p1, r2[0:SUBLANES]))
    yh = w0 * h2 + w1 * h1 + w2 * xh
    return jnp.concatenate([yh, y[SUBLANES:t]], axis=0)


def _layer_body(seq, x, p, prev_a, prev_f, refs):
    (g_mix, w_in, w_conv_a, w_out_a, g_v, w_sp, b_sp, w_out_b, w_o, g_ffn, w_up, w_conv_ffn,
     w_down, g_ple, w_ple_gate, w_ple, g_final) = refs
    t = x.shape[0]
    xn = _rms(x, g_mix[...]).astype(bf16)

    xa = _wdot(xn, w_in, c0=_XA, c1=_XA + D_A)
    ca = _wdot(xn, w_in, c0=_CA, c1=_CA + D_A)
    cx = ca * xa
    wca = w_conv_a[...]
    if seq:
        conv_a = _conv_seq(cx, wca, prev_a[6:7], prev_a[7:8])
    else:
        conv_a = wca[0:1] * prev_a[0] + wca[1:2] * prev_a[1] + wca[2:3] * cx
    ba = _wdot(xn, w_in, c0=_BA, c1=_BA + D_A)
    y_a = _wdot((ba * conv_a).astype(bf16), w_out_a)

    u = jax.nn.gelu(_wdot(xn, w_in, c0=_U, c1=_U + D_B))
    v = jax.nn.gelu(_wdot(xn, w_in, c0=_V, c1=_V + D_B))
    v = _rms(v, g_v[...])
    if seq:
        vb = v.astype(bf16)
        ti = lax.broadcasted_iota(jnp.int32, (CHUNK, CHUNK), 0)
        si = lax.broadcasted_iota(jnp.int32, (CHUNK, CHUNK), 1)
        causal = si <= ti
        bias = b_sp[...]
        cols = []
        for g in range(G_B):
            wg = jnp.where(causal, w_sp[g], jnp.zeros((), bf16))
            bg = bias[:, g:g + 1]
            rows = [_dot(wg, vb[c * CHUNK:(c + 1) * CHUNK, g * DG_B:(g + 1) * DG_B]) + bg
                    for c in range(t // CHUNK)]
            cols.append(jnp.concatenate(rows, axis=0))
        s = jnp.concatenate(cols, axis=1)
    else:
        s = w_sp[...] * v + b_sp[...]
    y_b = _wdot((u * s).astype(bf16), w_out_b)

    ga = _wdot(xn, w_in, c0=_GA, c1=_GA + D_MODEL)
    gb = _wdot(xn, w_in, c0=_GB, c1=_GB + D_MODEL)
    m = jax.nn.sigmoid(ga) * y_a + jax.nn.sigmoid(gb) * y_b
    h = x + _wdot(m.astype(bf16), w_o)

    xn2 = _rms(h, g_ffn[...]).astype(bf16)
    acc = None
    up_chunks = []
    for c0 in range(0, D_FF, FF_CHUNK):
        halves = []
        for off in (c0, D_FF + c0):
            up = _wdot(xn2, w_up, c0=off, c1=off + FF_CHUNK)
            wcf = w_conv_ffn[:, off:off + FF_CHUNK]
            if seq:
                conv = _conv_seq(up, wcf, prev_f[6:7, off:off + FF_CHUNK],
                                 prev_f[7:8, off:off + FF_CHUNK])
            else:
                conv = (wcf[0:1] * prev_f[0][:, off:off + FF_CHUNK]
                        + wcf[1:2] * prev_f[1][:, off:off + FF_CHUNK] + wcf[2:3] * up)
            up_chunks.append((off, up))
            halves.append(conv)
        hid = (jax.nn.gelu(halves[0]) * halves[1]).astype(bf16)
        part = _wdot(hid, w_down, r0=c0, r1=c0 + FF_CHUNK)
        acc = part if acc is None else acc + part
    h = h + acc

    gate = jax.nn.sigmoid(_wdot(_rms(h, g_ple[...]).astype(bf16), w_ple_gate))
    h = h + gate * _wdot(p.astype(bf16), w_ple)
    return _rms(h, g_final[...]), cx, v, up_chunks


def _prompt_kernel(x_ref, p_ref, *rest):
    refs = rest[:17]
    y_ref, ca_ref, cv_ref, cf_ref, carry_a, carry_f = rest[17:]
    l = pl.program_id(1)

    @pl.when(l == 0)
    def _():
        carry_a[...] = jnp.zeros_like(carry_a)
        carry_f[...] = jnp.zeros_like(carry_f)

    y, cx, v, up_chunks = _layer_body(True, x_ref[0], p_ref[0], carry_a[...], carry_f[...], refs)
    y_ref[0] = y
    carry_a[...] = cx[TM - SUBLANES:TM]
    for off, up in up_chunks:
        carry_f[:, off:off + FF_CHUNK] = up[TM - SUBLANES:TM]

    @pl.when(l == pl.num_programs(1) - 1)
    def _():
        ca_ref[0] = cx[TM - (CONV_W - 1):TM]
        cv_ref[0] = v[TM - CHUNK:TM]
        for off, up in up_chunks:
            cf_ref[0, :, off:off + FF_CHUNK] = up[TM - (CONV_W - 1):TM]


def _sample_kernel(x_ref, p_ref, sa_ref, sf_ref, *rest):
    refs = rest[:17]
    y_ref, ca_ref, cv_ref, cf_ref = rest[17:]
    sa0, sa1 = sa_ref[:, 0:D_A], sa_ref[:, D_A:2 * D_A]
    sf0, sf1 = sf_ref[:, 0:2 * D_FF], sf_ref[:, 2 * D_FF:4 * D_FF]
    y, cx, v, up_chunks = _layer_body(False, x_ref[...], p_ref[...], (sa0, sa1), (sf0, sf1), refs)
    y_ref[...] = y
    ca_ref[:, 0:D_A] = sa1
    ca_ref[:, D_A:2 * D_A] = cx
    cv_ref[...] = v
    cf_ref[:, 0:2 * D_FF] = sf1
    for off, up in up_chunks:
        cf_ref[:, 2 * D_FF + off:2 * D_FF + off + FF_CHUNK] = up


def _full(shape):
    return pl.BlockSpec(shape, lambda *_: (0,) * len(shape))


def kernel(x_prompt, x_sample, p_prompt, p_sample, state_conv_a, state_conv_ffn, g_mix, w_in, w_conv_a, w_out_a, g_v, w_spatial, b_spatial, w_out_b, w_o, g_ffn, w_up, w_conv_ffn, w_down, g_ple, w_ple_gate, w_ple, g_final):
    depth = g_mix.shape[0]
    assert depth == 1
    batch, seq_len, _ = x_prompt.shape
    dec_batch, dec_seq, _ = x_sample.shape
    assert dec_seq == 1 and seq_len % TM == 0 and TM % CHUNK == 0

    row = lambda a: a.reshape(1, -1)
    shared = dict(
        g_mix=row(g_mix[0]), w_in=_pack_rows(w_in[0]), w_conv_a=w_conv_a[0],
        w_out_a=_pack_rows(w_out_a[0]), g_v=row(g_v[0]), w_out_b=_pack_rows(w_out_b[0]),
        w_o=_pack_rows(w_o[0]), g_ffn=row(g_ffn[0]), w_up=_pack_rows(w_up[0]),
        w_conv_ffn=w_conv_ffn[0], w_down=_pack_rows(w_down[0]), g_ple=row(g_ple[0]),
        w_ple_gate=_pack_rows(w_ple_gate[0]), w_ple=_pack_rows(w_ple[0]), g_final=row(g_final))
    order = ("g_mix", "w_in", "w_conv_a", "w_out_a", "g_v", "w_sp", "b_sp", "w_out_b", "w_o",
             "g_ffn", "w_up", "w_conv_ffn", "w_down", "g_ple", "w_ple_gate", "w_ple", "g_final")

    pw = dict(shared, w_sp=w_spatial[0].astype(bf16), b_sp=b_spatial[0].T)
    pw_args = [pw[k] for k in order]
    n_l = seq_len // TM
    y_p, ca_p, cv_p, cf_p = pl.pallas_call(
        _prompt_kernel,
        grid=(batch, n_l),
        in_specs=[pl.BlockSpec((1, TM, D_MODEL), lambda b, l: (b, l, 0)),
                  pl.BlockSpec((1, TM, P_DIM), lambda b, l: (b, l, 0))]
                 + [_full(a.shape) for a in pw_args],
        out_specs=[pl.BlockSpec((1, TM, D_MODEL), lambda b, l: (b, l, 0)),
                   pl.BlockSpec((1, CONV_W - 1, D_A), lambda b, l: (b, 0, 0)),
                   pl.BlockSpec((1, CHUNK, D_B), lambda b, l: (b, 0, 0)),
                   pl.BlockSpec((1, CONV_W - 1, 2 * D_FF), lambda b, l: (b, 0, 0))],
        out_shape=[jax.ShapeDtypeStruct((batch, seq_len, D_MODEL), f32),
                   jax.ShapeDtypeStruct((batch, CONV_W - 1, D_A), f32),
                   jax.ShapeDtypeStruct((batch, CHUNK, D_B), f32),
                   jax.ShapeDtypeStruct((batch, CONV_W - 1, 2 * D_FF), f32)],
        scratch_shapes=[pltpu.VMEM((SUBLANES, D_A), f32), pltpu.VMEM((SUBLANES, 2 * D_FF), f32)],
        compiler_params=pltpu.CompilerParams(
            dimension_semantics=("arbitrary", "arbitrary"), vmem_limit_bytes=VMEM_LIMIT_BYTES),
        name="layer_prompt",
    )(x_prompt, p_prompt[0], *pw_args)

    sw = dict(shared,
              w_sp=row(jnp.repeat(w_spatial[0, :, 0, 0], DG_B)),
              b_sp=row(jnp.repeat(b_spatial[0, :, 0], DG_B)))
    sw_args = [sw[k] for k in order]
    s_in = [x_sample.reshape(dec_batch, D_MODEL), p_sample[0].reshape(dec_batch, P_DIM),
            state_conv_a[0].reshape(dec_batch, (CONV_W - 1) * D_A),
            state_conv_ffn[0].reshape(dec_batch, (CONV_W - 1) * 2 * D_FF)]
    y_s, ca_s, cv_s, cf_s = pl.pallas_call(
        _sample_kernel,
        in_specs=[_full(a.shape) for a in s_in + sw_args],
        out_specs=[_full((dec_batch, D_MODEL)), _full((dec_batch, (CONV_W - 1) * D_A)),
                   _full((dec_batch, D_B)), _full((dec_batch, (CONV_W - 1) * 2 * D_FF))],
        out_shape=[jax.ShapeDtypeStruct((dec_batch, D_MODEL), f32),
                   jax.ShapeDtypeStruct((dec_batch, (CONV_W - 1) * D_A), f32),
                   jax.ShapeDtypeStruct((dec_batch, D_B), f32),
                   jax.ShapeDtypeStruct((dec_batch, (CONV_W - 1) * 2 * D_FF), f32)],
        compiler_params=pltpu.CompilerParams(vmem_limit_bytes=VMEM_LIMIT_BYTES),
        name="layer_sample",
    )(*s_in, *sw_args)

    return (y_p, y_s.reshape(dec_batch, 1, D_MODEL),
            ca_p[None], ca_s.reshape(1, dec_batch, CONV_W - 1, D_A),
            cv_p[None], cv_s.reshape(1, dec_batch, 1, D_B),
            cf_p[None], cf_s.reshape(1, dec_batch, CONV_W - 1, 2 * D_FF))
```

```python
import jax
import jax.numpy as jnp
from jax import lax
from jax.experimental import pallas as pl
from jax.experimental.pallas import tpu as pltpu

D_MODEL = 1024
D_A = 1024
D_B = 1024
G_B = 8
DG_B = D_B // G_B
CHUNK = 128
CONV_W = 3
D_FF = 2816
P_DIM = 256
EPS = 1e-6

SUBLANES = 8
TM = 256
FF_CHUNK = 1408
PACK_STEPS = 16
VMEM_LIMIT_BYTES = 62 * 1024 * 1024

_XA, _BA, _CA, _U, _V, _GA, _GB = (0, D_A, 2 * D_A, 3 * D_A, 3 * D_A + D_B,
                                   3 * D_A + 2 * D_B, 3 * D_A + 2 * D_B + D_MODEL)

bf16 = jnp.bfloat16
f32 = jnp.float32


def _pack_kernel(*refs):
    n = len(refs) // 2
    for w_ref, o_ref in zip(refs[:n], refs[n:]):
        o_ref[...] = pltpu.bitcast(w_ref[...].astype(bf16), jnp.uint32)


def _pack_weights(ws):
    for w in ws:
        assert w.shape[0] % (2 * SUBLANES * PACK_STEPS) == 0
    blocks = [(w.shape[0] // PACK_STEPS, w.shape[1]) for w in ws]
    return pl.pallas_call(
        _pack_kernel,
        grid=(PACK_STEPS,),
        in_specs=[pl.BlockSpec(b, lambda i: (i, 0)) for b in blocks],
        out_specs=[pl.BlockSpec((b[0] // 2, b[1]), lambda i: (i, 0)) for b in blocks],
        out_shape=[jax.ShapeDtypeStruct((w.shape[0] // 2, w.shape[1]), jnp.uint32) for w in ws],
        compiler_params=pltpu.CompilerParams(
            dimension_semantics=("arbitrary",), vmem_limit_bytes=VMEM_LIMIT_BYTES),
        name="pack_weights",
    )(*ws)


def _rms(x, g):
    ms = jnp.mean(x * x, axis=-1, keepdims=True)
    return x * lax.rsqrt(ms + EPS) * g


def _dot(a, b):
    return jnp.dot(a, b, preferred_element_type=f32)


def _wdot(a, w_ref, r0=0, r1=None, c0=0, c1=None):
    r1 = 2 * w_ref.shape[0] if r1 is None else r1
    c1 = w_ref.shape[1] if c1 is None else c1
    w = pltpu.bitcast(w_ref[r0 // 2:r1 // 2, c0:c1], bf16)
    return _dot(a, w)


def _conv_seq(x, w, p0, p1):
    t = x.shape[0]
    r1 = pltpu.roll(x, 1, axis=0)
    r2 = pltpu.roll(x, 2, axis=0)
    w0, w1, w2 = w[0:1], w[1:2], w[2:3]
    y = w0 * r2 + w1 * r1 + w2 * x
    row = lax.broadcasted_iota(jnp.int32, (SUBLANES, 1), 0)
    xh = x[0:SUBLANES]
    h1 = jnp.where(row == 0, p1, r1[0:SUBLANES])
    h2 = jnp.where(row == 0, p0, jnp.where(row == 1, p1, r2[0:SUBLANES]))
    yh = w0 * h2 + w1 * h1 + w2 * xh
    return jnp.concatenate([yh, y[SUBLANES:t]], axis=0)


def _layer_body(seq, x, p, prev_a, prev_f, refs):
    (g_mix, w_in, w_conv_a, w_out_a, g_v, w_sp, b_sp, w_out_b, w_o, g_ffn, w_up, w_conv_ffn,
     w_down, g_ple, w_ple_gate, w_ple, g_final) = refs
    t = x.shape[0]
    xn = _rms(x, g_mix[...]).astype(bf16)

    xa = _wdot(xn, w_in, c0=_XA, c1=_XA + D_A)
    ca = _wdot(xn, w_in, c0=_CA, c1=_CA + D_A)
    cx = ca * xa
    wca = w_conv_a[...]
    if seq:
        conv_a = _conv_seq(cx, wca, prev_a[6:7], prev_a[7:8])
    else:
        conv_a = wca[0:1] * prev_a[0] + wca[1:2] * prev_a[1] + wca[2:3] * cx
    ba = _wdot(xn, w_in, c0=_BA, c1=_BA + D_A)
    y_a = _wdot((ba * conv_a).astype(bf16), w_out_a)

    u = jax.nn.gelu(_wdot(xn, w_in, c0=_U, c1=_U + D_B))
    v = jax.nn.gelu(_wdot(xn, w_in, c0=_V, c1=_V + D_B))
    v = _rms(v, g_v[...])
    if seq:
        vb = v.astype(bf16)
        ti = lax.broadcasted_iota(jnp.int32, (CHUNK, CHUNK), 0)
        si = lax.broadcasted_iota(jnp.int32, (CHUNK, CHUNK), 1)
        causal = si <= ti
        bias = b_sp[...]
        cols = []
        for g in range(G_B):
            wg = jnp.where(causal, w_sp[g], jnp.zeros((), bf16))
            bg = bias[:, g:g + 1]
            rows = [_dot(wg, vb[c * CHUNK:(c + 1) * CHUNK, g * DG_B:(g + 1) * DG_B]) + bg
                    for c in range(t // CHUNK)]
            cols.append(jnp.concatenate(rows, axis=0))
        s = jnp.concatenate(cols, axis=1)
    else:
        s = w_sp[...] * v + b_sp[...]
    y_b = _wdot((u * s).astype(bf16), w_out_b)

    ga = _wdot(xn, w_in, c0=_GA, c1=_GA + D_MODEL)
    gb = _wdot(xn, w_in, c0=_GB, c1=_GB + D_MODEL)
    m = jax.nn.sigmoid(ga) * y_a + jax.nn.sigmoid(gb) * y_b
    h = x + _wdot(m.astype(bf16), w_o)

    xn2 = _rms(h, g_ffn[...]).astype(bf16)
    acc = None
    up_chunks = []
    for c0 in range(0, D_FF, FF_CHUNK):
        halves = []
        for off in (c0, D_FF + c0):
            up = _wdot(xn2, w_up, c0=off, c1=off + FF_CHUNK)
            wcf = w_conv_ffn[:, off:off + FF_CHUNK]
            if seq:
                conv = _conv_seq(up, wcf, prev_f[6:7, off:off + FF_CHUNK],
                                 prev_f[7:8, off:off + FF_CHUNK])
            else:
                conv = (wcf[0:1] * prev_f[0][:, off:off + FF_CHUNK]
                        + wcf[1:2] * prev_f[1][:, off:off + FF_CHUNK] + wcf[2:3] * up)
            up_chunks.append((off, up))
            halves.append(conv)
        hid = (jax.nn.gelu(halves[0]) * halves[1]).astype(bf16)
        part = _wdot(hid, w_down, r0=c0, r1=c0 + FF_CHUNK)
        acc = part if acc is None else acc + part
    h = h + acc

    gate = jax.nn.sigmoid(_wdot(_rms(h, g_ple[...]).astype(bf16), w_ple_gate))
    h = h + gate * _wdot(p.astype(bf16), w_ple)
    return _rms(h, g_final[...]), cx, v, up_chunks


def _prompt_kernel(x_ref, p_ref, *rest):
    refs = rest[:17]
    y_ref, ca_ref, cv_ref, cf_ref, carry_a, carry_f = rest[17:]
    l = pl.program_id(1)

    @pl.when(l == 0)
    def _():
        carry_a[...] = jnp.zeros_like(carry_a)
        carry_f[...] = jnp.zeros_like(carry_f)

    y, cx, v, up_chunks = _layer_body(True, x_ref[0], p_ref[0], carry_a[...], carry_f[...], refs)
    y_ref[0] = y
    carry_a[...] = cx[TM - SUBLANES:TM]
    for off, up in up_chunks:
        carry_f[:, off:off + FF_CHUNK] = up[TM - SUBLANES:TM]

    @pl.when(l == pl.num_programs(1) - 1)
    def _():
        ca_ref[0] = cx[TM - (CONV_W - 1):TM]
        cv_ref[0] = v[TM - CHUNK:TM]
        for off, up in up_chunks:
            cf_ref[0, :, off:off + FF_CHUNK] = up[TM - (CONV_W - 1):TM]


def _sample_kernel(x_ref, p_ref, sa_ref, sf_ref, *rest):
    refs = rest[:17]
    y_ref, ca_ref, cv_ref, cf_ref = rest[17:]
    sa0, sa1 = sa_ref[:, 0:D_A], sa_ref[:, D_A:2 * D_A]
    sf0, sf1 = sf_ref[:, 0:2 * D_FF], sf_ref[:, 2 * D_FF:4 * D_FF]
    y, cx, v, up_chunks = _layer_body(False, x_ref[...], p_ref[...], (sa0, sa1), (sf0, sf1), refs)
    y_ref[...] = y
    ca_ref[:, 0:D_A] = sa1
    ca_ref[:, D_A:2 * D_A] = cx
    cv_ref[...] = v
    cf_ref[:, 0:2 * D_FF] = sf1
    for off, up in up_chunks:
        cf_ref[:, 2 * D_FF + off:2 * D_FF + off + FF_CHUNK] = up


def _full(shape):
    return pl.BlockSpec(shape, lambda *_: (0,) * len(shape))


def kernel(x_prompt, x_sample, p_prompt, p_sample, state_conv_a, state_conv_ffn, g_mix, w_in, w_conv_a, w_out_a, g_v, w_spatial, b_spatial, w_out_b, w_o, g_ffn, w_up, w_conv_ffn, w_down, g_ple, w_ple_gate, w_ple, g_final):
    depth = g_mix.shape[0]
    assert depth == 1
    batch, seq_len, _ = x_prompt.shape
    dec_batch, dec_seq, _ = x_sample.shape
    assert dec_seq == 1 and seq_len % TM == 0 and TM % CHUNK == 0

    row = lambda a: a.reshape(1, -1)
    mats = ("w_in", "w_out_a", "w_out_b", "w_o", "w_up", "w_down", "w_ple_gate", "w_ple")
    packed = _pack_weights([w_in[0], w_out_a[0], w_out_b[0], w_o[0], w_up[0], w_down[0],
                            w_ple_gate[0], w_ple[0]])
    shared = dict(
        zip(mats, packed),
        g_mix=row(g_mix[0]), w_conv_a=w_conv_a[0], g_v=row(g_v[0]), g_ffn=row(g_ffn[0]),
        w_conv_ffn=w_conv_ffn[0], g_ple=row(g_ple[0]), g_final=row(g_final))
    order = ("g_mix", "w_in", "w_conv_a", "w_out_a", "g_v", "w_sp", "b_sp", "w_out_b", "w_o",
             "g_ffn", "w_up", "w_conv_ffn", "w_down", "g_ple", "w_ple_gate", "w_ple", "g_final")

    pw = dict(shared, w_sp=w_spatial[0].astype(bf16), b_sp=b_spatial[0].T)
    pw_args = [pw[k] for k in order]
    n_l = seq_len // TM
    y_p, ca_p, cv_p, cf_p = pl.pallas_call(
        _prompt_kernel,
        grid=(batch, n_l),
        in_specs=[pl.BlockSpec((1, TM, D_MODEL), lambda b, l: (b, l, 0)),
                  pl.BlockSpec((1, TM, P_DIM), lambda b, l: (b, l, 0))]
                 + [_full(a.shape) for a in pw_args],
        out_specs=[pl.BlockSpec((1, TM, D_MODEL), lambda b, l: (b, l, 0)),
                   pl.BlockSpec((1, CONV_W - 1, D_A), lambda b, l: (b, 0, 0)),
                   pl.BlockSpec((1, CHUNK, D_B), lambda b, l: (b, 0, 0)),
                   pl.BlockSpec((1, CONV_W - 1, 2 * D_FF), lambda b, l: (b, 0, 0))],
        out_shape=[jax.ShapeDtypeStruct((batch, seq_len, D_MODEL), f32),
                   jax.ShapeDtypeStruct((batch, CONV_W - 1, D_A), f32),
                   jax.ShapeDtypeStruct((batch, CHUNK, D_B), f32),
                   jax.ShapeDtypeStruct((batch, CONV_W - 1, 2 * D_FF), f32)],
        scratch_shapes=[pltpu.VMEM((SUBLANES, D_A), f32), pltpu.VMEM((SUBLANES, 2 * D_FF), f32)],
        compiler_params=pltpu.CompilerParams(
            dimension_semantics=("arbitrary", "arbitrary"), vmem_limit_bytes=VMEM_LIMIT_BYTES),
        name="layer_prompt",
    )(x_prompt, p_prompt[0], *pw_args)

    sw = dict(shared,
              w_sp=row(jnp.repeat(w_spatial[0, :, 0, 0], DG_B)),
              b_sp=row(jnp.repeat(b_spatial[0, :, 0], DG_B)))
    sw_args = [sw[k] for k in order]
    s_in = [x_sample.reshape(dec_batch, D_MODEL), p_sample[0].reshape(dec_batch, P_DIM),
            state_conv_a[0].reshape(dec_batch, (CONV_W - 1) * D_A),
            state_conv_ffn[0].reshape(dec_batch, (CONV_W - 1) * 2 * D_FF)]
    y_s, ca_s, cv_s, cf_s = pl.pallas_call(
        _sample_kernel,
        in_specs=[_full(a.shape) for a in s_in + sw_args],
        out_specs=[_full((dec_batch, D_MODEL)), _full((dec_batch, (CONV_W - 1) * D_A)),
                   _full((dec_batch, D_B)), _full((dec_batch, (CONV_W - 1) * 2 * D_FF))],
        out_shape=[jax.ShapeDtypeStruct((dec_batch, D_MODEL), f32),
                   jax.ShapeDtypeStruct((dec_batch, (CONV_W - 1) * D_A), f32),
                   jax.ShapeDtypeStruct((dec_batch, D_B), f32),
                   jax.ShapeDtypeStruct((dec_batch, (CONV_W - 1) * 2 * D_FF), f32)],
        compiler_params=pltpu.CompilerParams(vmem_limit_bytes=VMEM_LIMIT_BYTES),
        name="layer_sample",
    )(*s_in, *sw_args)

    return (y_p, y_s.reshape(dec_batch, 1, D_MODEL),
            ca_p[None], ca_s.reshape(1, dec_batch, CONV_W - 1, D_A),
            cv_p[None], cv_s.reshape(1, dec_batch, 1, D_B),
            cf_p[None], cf_s.reshape(1, dec_batch, CONV_W - 1, 2 * D_FF))
```

```python
import functools

import jax
import jax.numpy as jnp
from jax import lax
from jax.experimental import pallas as pl
from jax.experimental.pallas import tpu as pltpu

D_MODEL = 1024
D_A = 1024
D_B = 1024
G_B = 8
DG_B = D_B // G_B
CHUNK = 128
CONV_W = 3
D_FF = 2816
P_DIM = 256
EPS = 1e-6

SUBLANES = 8
LANES = 128
TM = 256
FF_CHUNK = 1408
PACK_STEPS = 16
VMEM_LIMIT_BYTES = 62 * 1024 * 1024

_XA, _BA, _CA, _U, _V, _GA, _GB = (0, D_A, 2 * D_A, 3 * D_A, 3 * D_A + D_B,
                                   3 * D_A + 2 * D_B, 3 * D_A + 2 * D_B + D_MODEL)

bf16 = jnp.bfloat16
f32 = jnp.float32


def _pack_kernel(*refs):
    n = len(refs) // 2
    for w_ref, o_ref in zip(refs[:n], refs[n:]):
        o_ref[...] = pltpu.bitcast(w_ref[...].astype(bf16), jnp.uint32)


def _pack_weights(ws):
    for w in ws:
        assert w.shape[0] % (2 * SUBLANES * PACK_STEPS) == 0
    blocks = [(w.shape[0] // PACK_STEPS, w.shape[1]) for w in ws]
    return pl.pallas_call(
        _pack_kernel,
        grid=(PACK_STEPS,),
        in_specs=[pl.BlockSpec(b, lambda i: (i, 0)) for b in blocks],
        out_specs=[pl.BlockSpec((b[0] // 2, b[1]), lambda i: (i, 0)) for b in blocks],
        out_shape=[jax.ShapeDtypeStruct((w.shape[0] // 2, w.shape[1]), jnp.uint32) for w in ws],
        compiler_params=pltpu.CompilerParams(
            dimension_semantics=("arbitrary",), vmem_limit_bytes=VMEM_LIMIT_BYTES),
        name="pack_weights",
    )(*ws)


def _rms(x, g):
    ms = jnp.mean(x * x, axis=-1, keepdims=True)
    return x * lax.rsqrt(ms + EPS) * g


def _dot(a, b):
    return jnp.dot(a, b, preferred_element_type=f32)


def _wdot(a, w_ref, r0=0, r1=None, c0=0, c1=None):
    r1 = 2 * w_ref.shape[0] if r1 is None else r1
    c1 = w_ref.shape[1] if c1 is None else c1
    w = pltpu.bitcast(w_ref[r0 // 2:r1 // 2, c0:c1], bf16)
    return _dot(a, w)


def _conv_seq(x, w, slab, col0, first):
    t, c = x.shape
    outs = []
    for j in range(c // LANES):
        jj = col0 // LANES + j
        xj = x[:, j * LANES:(j + 1) * LANES]
        wj = w[:, j * LANES:(j + 1) * LANES]
        slab[jj, 0:SUBLANES, :] = jnp.where(first, 0.0, slab[jj, t:t + SUBLANES, :])
        slab[jj, SUBLANES:SUBLANES + t, :] = xj
        outs.append(wj[0:1] * slab[jj, SUBLANES - 2:SUBLANES - 2 + t, :]
                    + wj[1:2] * slab[jj, SUBLANES - 1:SUBLANES - 1 + t, :] + wj[2:3] * xj)
    return jnp.concatenate(outs, axis=1)


def _mixers(seq, x, prev_a, refs, out):
    (g_mix, w_in, w_conv_a, w_out_a, g_v, w_sp, b_sp, w_out_b, w_o) = refs
    t = x.shape[0]
    xn = _rms(x, g_mix[...]).astype(bf16)
    yield

    v = jax.nn.gelu(_wdot(xn, w_in, c0=_V, c1=_V + D_B))
    u = jax.nn.gelu(_wdot(xn, w_in, c0=_U, c1=_U + D_B))
    v = _rms(v, g_v[...])
    out["v"] = v
    yield

    if seq:
        vb = v.astype(bf16)
        ti = lax.broadcasted_iota(jnp.int32, (CHUNK, CHUNK), 0)
        si = lax.broadcasted_iota(jnp.int32, (CHUNK, CHUNK), 1)
        causal = si <= ti
        bias = b_sp[...]
        cols = []
        for g in range(G_B):
            wg = jnp.where(causal, w_sp[g], jnp.zeros((), bf16))
            bg = bias[:, g:g + 1]
            rows = [_dot(wg, vb[c * CHUNK:(c + 1) * CHUNK, g * DG_B:(g + 1) * DG_B]) + bg
                    for c in range(t // CHUNK)]
            cols.append(jnp.concatenate(rows, axis=0))
        s = jnp.concatenate(cols, axis=1)
    else:
        s = w_sp[...] * v + b_sp[...]
    us = (u * s).astype(bf16)

    xa = _wdot(xn, w_in, c0=_XA, c1=_XA + D_A)
    ca = _wdot(xn, w_in, c0=_CA, c1=_CA + D_A)
    cx = ca * xa
    out["cx"] = cx
    wca = w_conv_a[...]
    if seq:
        conv_a = _conv_seq(cx, wca, prev_a[0], 0, prev_a[1])
    else:
        conv_a = wca[0:1] * prev_a[0] + wca[1:2] * prev_a[1] + wca[2:3] * cx
    ba = _wdot(xn, w_in, c0=_BA, c1=_BA + D_A)
    yield

    y_a = _wdot((ba * conv_a).astype(bf16), w_out_a)
    y_b = _wdot(us, w_out_b)
    ga = _wdot(xn, w_in, c0=_GA, c1=_GA + D_MODEL)
    gb = _wdot(xn, w_in, c0=_GB, c1=_GB + D_MODEL)
    m = jax.nn.sigmoid(ga) * y_a + jax.nn.sigmoid(gb) * y_b
    yield

    out["h"] = x + _wdot(m.astype(bf16), w_o)
    yield


def _ffn_ple(seq, h, p, prev_f, refs, out):
    (g_ffn, w_up, w_conv_ffn, w_down, g_ple, w_ple_gate, w_ple, g_final) = refs
    pe = _wdot(p.astype(bf16), w_ple)
    xn2 = _rms(h, g_ffn[...]).astype(bf16)
    yield

    acc = None
    out["up"] = []
    for c0 in range(0, D_FF, FF_CHUNK):
        halves = []
        for off in (c0, D_FF + c0):
            up = _wdot(xn2, w_up, c0=off, c1=off + FF_CHUNK)
            wcf = w_conv_ffn[:, off:off + FF_CHUNK]
            if seq:
                conv = _conv_seq(up, wcf, prev_f[0], off, prev_f[1])
            else:
                conv = (wcf[0:1] * prev_f[0][:, off:off + FF_CHUNK]
                        + wcf[1:2] * prev_f[1][:, off:off + FF_CHUNK] + wcf[2:3] * up)
            out["up"].append((off, up))
            halves.append(conv)
        hid = (jax.nn.gelu(halves[0]) * halves[1]).astype(bf16)
        yield
        part = _wdot(hid, w_down, r0=c0, r1=c0 + FF_CHUNK)
        acc = part if acc is None else acc + part
        yield
    h = h + acc

    gate = jax.nn.sigmoid(_wdot(_rms(h, g_ple[...]).astype(bf16), w_ple_gate))
    yield
    h = h + gate * pe
    out["y"] = _rms(h, g_final[...])
    yield


def _run(schedule, **gens):
    for name in schedule:
        next(gens[name])
    for g in gens.values():
        for _ in g:
            pass


PIPELINE_ORDER = "FMFMFFMFMFMF"

N_MIX_REFS = 9
N_FFN_REFS = 8


def _prompt_kernel(tiles_per_seq, x_ref, p_ref, *rest):
    mix_refs = rest[:N_MIX_REFS]
    ffn_refs = rest[N_MIX_REFS:N_MIX_REFS + N_FFN_REFS]
    y_ref, ca_ref, cv_ref, cf_ref, slab_a, slab_f, h_scr = rest[N_MIX_REFS + N_FFN_REFS:]
    i = pl.program_id(0)
    last = pl.num_programs(0) - 1

    @pl.when(i == 0)
    def _():
        slab_a[:, TM:TM + SUBLANES, :] = jnp.zeros((slab_a.shape[0], SUBLANES, LANES), f32)
        slab_f[:, TM:TM + SUBLANES, :] = jnp.zeros((slab_f.shape[0], SUBLANES, LANES), f32)
        h_scr[...] = jnp.zeros_like(h_scr)

    def ffn_of_previous_tile(out):
        first = (jnp.maximum(i - 1, 0) % tiles_per_seq) == 0
        return _ffn_ple(True, h_scr[...], p_ref[0], (slab_f, first), ffn_refs, out)

    def store_ffn(out):
        y_ref[0] = out["y"]
        for off, up in out["up"]:
            cf_ref[0, :, off:off + FF_CHUNK] = up[TM - (CONV_W - 1):TM]

    @pl.when(i < last)
    def _():
        first = (i % tiles_per_seq) == 0
        mo, fo = {}, {}
        _run(PIPELINE_ORDER, M=_mixers(True, x_ref[0], (slab_a, first), mix_refs, mo),
             F=ffn_of_previous_tile(fo))
        store_ffn(fo)
        h_scr[...] = mo["h"]
        cx, v = mo["cx"], mo["v"]
        ca_ref[0] = cx[TM - (CONV_W - 1):TM]
        cv_ref[0] = v[TM - CHUNK:TM]

    @pl.when(i == last)
    def _():
        fo = {}
        _run("", F=ffn_of_previous_tile(fo))
        store_ffn(fo)


def _sample_kernel(x_ref, p_ref, sa_ref, sf_ref, *rest):
    mix_refs = rest[:N_MIX_REFS]
    ffn_refs = rest[N_MIX_REFS:N_MIX_REFS + N_FFN_REFS]
    y_ref, ca_ref, cv_ref, cf_ref = rest[N_MIX_REFS + N_FFN_REFS:]
    sa0, sa1 = sa_ref[:, 0:D_A], sa_ref[:, D_A:2 * D_A]
    sf0, sf1 = sf_ref[:, 0:2 * D_FF], sf_ref[:, 2 * D_FF:4 * D_FF]
    mo, fo = {}, {}
    _run("", M=_mixers(False, x_ref[...], (sa0, sa1), mix_refs, mo))
    _run("", F=_ffn_ple(False, mo["h"], p_ref[...], (sf0, sf1), ffn_refs, fo))
    y_ref[...] = fo["y"]
    ca_ref[:, 0:D_A] = sa1
    ca_ref[:, D_A:2 * D_A] = mo["cx"]
    cv_ref[...] = mo["v"]
    cf_ref[:, 0:2 * D_FF] = sf1
    for off, up in fo["up"]:
        cf_ref[:, 2 * D_FF + off:2 * D_FF + off + FF_CHUNK] = up


def _full(shape):
    return pl.BlockSpec(shape, lambda *_: (0,) * len(shape))


def kernel(x_prompt, x_sample, p_prompt, p_sample, state_conv_a, state_conv_ffn, g_mix, w_in, w_conv_a, w_out_a, g_v, w_spatial, b_spatial, w_out_b, w_o, g_ffn, w_up, w_conv_ffn, w_down, g_ple, w_ple_gate, w_ple, g_final):
    depth = g_mix.shape[0]
    assert depth == 1
    batch, seq_len, _ = x_prompt.shape
    dec_batch, dec_seq, _ = x_sample.shape
    assert dec_seq == 1 and seq_len % TM == 0 and TM % CHUNK == 0

    row = lambda a: a.reshape(1, -1)
    mats = ("w_in", "w_out_a", "w_out_b", "w_o", "w_up", "w_down", "w_ple_gate", "w_ple")
    packed = _pack_weights([w_in[0], w_out_a[0], w_out_b[0], w_o[0], w_up[0], w_down[0],
                            w_ple_gate[0], w_ple[0]])
    shared = dict(
        zip(mats, packed),
        g_mix=row(g_mix[0]), w_conv_a=w_conv_a[0], g_v=row(g_v[0]), g_ffn=row(g_ffn[0]),
        w_conv_ffn=w_conv_ffn[0], g_ple=row(g_ple[0]), g_final=row(g_final))
    order = ("g_mix", "w_in", "w_conv_a", "w_out_a", "g_v", "w_sp", "b_sp", "w_out_b", "w_o",
             "g_ffn", "w_up", "w_conv_ffn", "w_down", "g_ple", "w_ple_gate", "w_ple", "g_final")
    assert len(order) == N_MIX_REFS + N_FFN_REFS

    pw = dict(shared, w_sp=w_spatial[0].astype(bf16), b_sp=b_spatial[0].T)
    pw_args = [pw[k] for k in order]
    n_l = seq_len // TM
    n_tiles = batch * n_l

    def mix_tile(i):
        return jnp.minimum(i, n_tiles - 1)

    def ffn_tile(i):
        return jnp.maximum(i - 1, 0)

    y_p, ca_p, cv_p, cf_p = pl.pallas_call(
        functools.partial(_prompt_kernel, n_l),
        grid=(n_tiles + 1,),
        in_specs=[pl.BlockSpec((1, TM, D_MODEL), lambda i: (mix_tile(i) // n_l, mix_tile(i) % n_l, 0)),
                  pl.BlockSpec((1, TM, P_DIM), lambda i: (ffn_tile(i) // n_l, ffn_tile(i) % n_l, 0))]
                 + [_full(a.shape) for a in pw_args],
        out_specs=[pl.BlockSpec((1, TM, D_MODEL), lambda i: (ffn_tile(i) // n_l, ffn_tile(i) % n_l, 0)),
                   pl.BlockSpec((1, CONV_W - 1, D_A), lambda i: (mix_tile(i) // n_l, 0, 0)),
                   pl.BlockSpec((1, CHUNK, D_B), lambda i: (mix_tile(i) // n_l, 0, 0)),
                   pl.BlockSpec((1, CONV_W - 1, 2 * D_FF), lambda i: (ffn_tile(i) // n_l, 0, 0))],
        out_shape=[jax.ShapeDtypeStruct((batch, seq_len, D_MODEL), f32),
                   jax.ShapeDtypeStruct((batch, CONV_W - 1, D_A), f32),
                   jax.ShapeDtypeStruct((batch, CHUNK, D_B), f32),
                   jax.ShapeDtypeStruct((batch, CONV_W - 1, 2 * D_FF), f32)],
        scratch_shapes=[pltpu.VMEM((D_A // LANES, TM + SUBLANES, LANES), f32),
                        pltpu.VMEM((2 * D_FF // LANES, TM + SUBLANES, LANES), f32),
                        pltpu.VMEM((TM, D_MODEL), f32)],
        compiler_params=pltpu.CompilerParams(
            dimension_semantics=("arbitrary",), vmem_limit_bytes=VMEM_LIMIT_BYTES),
        name="layer_prompt",
    )(x_prompt, p_prompt[0], *pw_args)

    sw = dict(shared,
              w_sp=row(jnp.repeat(w_spatial[0, :, 0, 0], DG_B)),
              b_sp=row(jnp.repeat(b_spatial[0, :, 0], DG_B)))
    sw_args = [sw[k] for k in order]
    s_in = [x_sample.reshape(dec_batch, D_MODEL), p_sample[0].reshape(dec_batch, P_DIM),
            state_conv_a[0].reshape(dec_batch, (CONV_W - 1) * D_A),
            state_conv_ffn[0].reshape(dec_batch, (CONV_W - 1) * 2 * D_FF)]
    y_s, ca_s, cv_s, cf_s = pl.pallas_call(
        _sample_kernel,
        in_specs=[_full(a.shape) for a in s_in + sw_args],
        out_specs=[_full((dec_batch, D_MODEL)), _full((dec_batch, (CONV_W - 1) * D_A)),
                   _full((dec_batch, D_B)), _full((dec_batch, (CONV_W - 1) * 2 * D_FF))],
        out_shape=[jax.ShapeDtypeStruct((dec_batch, D_MODEL), f32),
                   jax.ShapeDtypeStruct((dec_batch, (CONV_W - 1) * D_A), f32),
                   jax.ShapeDtypeStruct((dec_batch, D_B), f32),
                   jax.ShapeDtypeStruct((dec_batch, (CONV_W - 1) * 2 * D_FF), f32)],
        compiler_params=pltpu.CompilerParams(vmem_limit_bytes=VMEM_LIMIT_BYTES),
        name="layer_sample",
    )(*s_in, *sw_args)

    return (y_p, y_s.reshape(dec_batch, 1, D_MODEL),
            ca_p[None], ca_s.reshape(1, dec_batch, CONV_W - 1, D_A),
            cv_p[None], cv_s.reshape(1, dec_batch, 1, D_B),
            cf_p[None], cf_s.reshape(1, dec_batch, CONV_W - 1, 2 * D_FF))
```

```python
import functools

import jax
import jax.numpy as jnp
from jax import lax
from jax.experimental import pallas as pl
from jax.experimental.pallas import tpu as pltpu

D_MODEL = 1024
D_A = 1024
D_B = 1024
G_B = 8
DG_B = D_B // G_B
CHUNK = 128
CONV_W = 3
D_FF = 2816
P_DIM = 256
EPS = 1e-6

SUBLANES = 8
LANES = 128
TM = 256
MXU_N = 256
FF_SPLITS = (0, 6 * MXU_N, D_FF)
PACK_STEPS = 16
VMEM_LIMIT_BYTES = 62 * 1024 * 1024

_XA, _BA, _CA, _U, _V, _GA, _GB = (0, D_A, 2 * D_A, 3 * D_A, 3 * D_A + D_B,
                                   3 * D_A + 2 * D_B, 3 * D_A + 2 * D_B + D_MODEL)

bf16 = jnp.bfloat16
f32 = jnp.float32


def _pack_kernel(*refs):
    n = len(refs) // 2
    for w_ref, o_ref in zip(refs[:n], refs[n:]):
        o_ref[...] = pltpu.bitcast(w_ref[...].astype(bf16), jnp.uint32)


def _pack_weights(ws):
    for w in ws:
        assert w.shape[0] % (2 * SUBLANES * PACK_STEPS) == 0
    blocks = [(w.shape[0] // PACK_STEPS, w.shape[1]) for w in ws]
    return pl.pallas_call(
        _pack_kernel,
        grid=(PACK_STEPS,),
        in_specs=[pl.BlockSpec(b, lambda i: (i, 0)) for b in blocks],
        out_specs=[pl.BlockSpec((b[0] // 2, b[1]), lambda i: (i, 0)) for b in blocks],
        out_shape=[jax.ShapeDtypeStruct((w.shape[0] // 2, w.shape[1]), jnp.uint32) for w in ws],
        compiler_params=pltpu.CompilerParams(
            dimension_semantics=("arbitrary",), vmem_limit_bytes=VMEM_LIMIT_BYTES),
        name="pack_weights",
    )(*ws)


def _rms(x, g):
    ms = jnp.mean(x * x, axis=-1, keepdims=True)
    return x * lax.rsqrt(ms + EPS) * g


def _dot(a, b):
    return jnp.dot(a, b, preferred_element_type=f32)


def _wdot(a, w_ref, r0=0, r1=None, c0=0, c1=None):
    r1 = 2 * w_ref.shape[0] if r1 is None else r1
    c1 = w_ref.shape[1] if c1 is None else c1
    w = pltpu.bitcast(w_ref[r0 // 2:r1 // 2, c0:c1], bf16)
    return _dot(a, w)


def _conv_seq(x, w, slab, col0, first):
    t, c = x.shape
    outs = []
    for j in range(c // LANES):
        jj = col0 // LANES + j
        xj = x[:, j * LANES:(j + 1) * LANES]
        wj = w[:, j * LANES:(j + 1) * LANES]
        slab[jj, 0:SUBLANES, :] = jnp.where(first, 0.0, slab[jj, t:t + SUBLANES, :])
        slab[jj, SUBLANES:SUBLANES + t, :] = xj
        outs.append(wj[0:1] * slab[jj, SUBLANES - 2:SUBLANES - 2 + t, :]
                    + wj[1:2] * slab[jj, SUBLANES - 1:SUBLANES - 1 + t, :] + wj[2:3] * xj)
    return jnp.concatenate(outs, axis=1)


def _mixers(seq, x, prev_a, refs, out):
    (g_mix, w_in, w_conv_a, w_out_a, g_v, w_sp, b_sp, w_out_b, w_o) = refs
    t = x.shape[0]
    xn = _rms(x, g_mix[...]).astype(bf16)
    yield

    v = jax.nn.gelu(_wdot(xn, w_in, c0=_V, c1=_V + D_B))
    u = jax.nn.gelu(_wdot(xn, w_in, c0=_U, c1=_U + D_B))
    v = _rms(v, g_v[...])
    out["v"] = v
    yield

    if seq:
        vb = v.astype(bf16)
        ti = lax.broadcasted_iota(jnp.int32, (CHUNK, CHUNK), 0)
        si = lax.broadcasted_iota(jnp.int32, (CHUNK, CHUNK), 1)
        causal = si <= ti
        bias = b_sp[...]
        n_c = t // CHUNK
        cols = []
        for g in range(G_B):
            wg = jnp.where(causal, w_sp[g], jnp.zeros((), bf16))
            bg = bias[:, g:g + 1]
            vg = jnp.concatenate([vb[c * CHUNK:(c + 1) * CHUNK, g * DG_B:(g + 1) * DG_B]
                                  for c in range(n_c)], axis=1)
            sg = _dot(wg, vg)
            cols.append(jnp.concatenate([sg[:, c * DG_B:(c + 1) * DG_B] + bg for c in range(n_c)],
                                        axis=0))
        s = jnp.concatenate(cols, axis=1)
    else:
        s = w_sp[...] * v + b_sp[...]
    us = (u * s).astype(bf16)

    xa = _wdot(xn, w_in, c0=_XA, c1=_XA + D_A)
    ca = _wdot(xn, w_in, c0=_CA, c1=_CA + D_A)
    cx = ca * xa
    out["cx"] = cx
    wca = w_conv_a[...]
    if seq:
        conv_a = _conv_seq(cx, wca, prev_a[0], 0, prev_a[1])
    else:
        conv_a = wca[0:1] * prev_a[0] + wca[1:2] * prev_a[1] + wca[2:3] * cx
    ba = _wdot(xn, w_in, c0=_BA, c1=_BA + D_A)
    yield

    y_a = _wdot((ba * conv_a).astype(bf16), w_out_a)
    y_b = _wdot(us, w_out_b)
    ga = _wdot(xn, w_in, c0=_GA, c1=_GA + D_MODEL)
    gb = _wdot(xn, w_in, c0=_GB, c1=_GB + D_MODEL)
    m = jax.nn.sigmoid(ga) * y_a + jax.nn.sigmoid(gb) * y_b
    yield

    out["h"] = x + _wdot(m.astype(bf16), w_o)
    yield


def _ffn_ple(seq, h, p, prev_f, refs, out):
    (g_ffn, w_up, w_conv_ffn, w_down, g_ple, w_ple_gate, w_ple, g_final) = refs
    pe = _wdot(p.astype(bf16), w_ple)
    xn2 = _rms(h, g_ffn[...]).astype(bf16)
    yield

    acc = None
    out["up"] = []
    for c0, c1 in zip(FF_SPLITS[:-1], FF_SPLITS[1:]):
        n = c1 - c0
        halves = []
        for off in (c0, D_FF + c0):
            up = _wdot(xn2, w_up, c0=off, c1=off + n)
            wcf = w_conv_ffn[:, off:off + n]
            if seq:
                conv = _conv_seq(up, wcf, prev_f[0], off, prev_f[1])
            else:
                conv = (wcf[0:1] * prev_f[0][:, off:off + n]
                        + wcf[1:2] * prev_f[1][:, off:off + n] + wcf[2:3] * up)
            out["up"].append((off, up))
            halves.append(conv)
        hid = (jax.nn.gelu(halves[0]) * halves[1]).astype(bf16)
        yield
        part = _wdot(hid, w_down, r0=c0, r1=c1)
        acc = part if acc is None else acc + part
        yield
    h = h + acc

    gate = jax.nn.sigmoid(_wdot(_rms(h, g_ple[...]).astype(bf16), w_ple_gate))
    yield
    h = h + gate * pe
    out["y"] = _rms(h, g_final[...])
    yield


def _run(schedule, **gens):
    for name in schedule:
        next(gens[name])
    for g in gens.values():
        for _ in g:
            pass


PIPELINE_ORDER = "FMFMFFMFMFMF"

N_MIX_REFS = 9
N_FFN_REFS = 8


def _prompt_kernel(tiles_per_seq, x_ref, p_ref, *rest):
    mix_refs = rest[:N_MIX_REFS]
    ffn_refs = rest[N_MIX_REFS:N_MIX_REFS + N_FFN_REFS]
    y_ref, ca_ref, cv_ref, cf_ref, slab_a, slab_f, h_scr = rest[N_MIX_REFS + N_FFN_REFS:]
    i = pl.program_id(0)
    last = pl.num_programs(0) - 1

    @pl.when(i == 0)
    def _():
        slab_a[:, TM:TM + SUBLANES, :] = jnp.zeros((slab_a.shape[0], SUBLANES, LANES), f32)
        slab_f[:, TM:TM + SUBLANES, :] = jnp.zeros((slab_f.shape[0], SUBLANES, LANES), f32)
        h_scr[...] = jnp.zeros_like(h_scr)

    def ffn_of_previous_tile(out):
        first = (jnp.maximum(i - 1, 0) % tiles_per_seq) == 0
        return _ffn_ple(True, h_scr[...], p_ref[0], (slab_f, first), ffn_refs, out)

    def store_ffn(out):
        y_ref[0] = out["y"]
        for off, up in out["up"]:
            cf_ref[0, :, off:off + up.shape[1]] = up[TM - (CONV_W - 1):TM]

    @pl.when(i < last)
    def _():
        first = (i % tiles_per_seq) == 0
        mo, fo = {}, {}
        _run(PIPELINE_ORDER, M=_mixers(True, x_ref[0], (slab_a, first), mix_refs, mo),
             F=ffn_of_previous_tile(fo))
        store_ffn(fo)
        h_scr[...] = mo["h"]
        cx, v = mo["cx"], mo["v"]
        ca_ref[0] = cx[TM - (CONV_W - 1):TM]
        cv_ref[0] = v[TM - CHUNK:TM]

    @pl.when(i == last)
    def _():
        fo = {}
        _run("", F=ffn_of_previous_tile(fo))
        store_ffn(fo)


def _sample_kernel(x_ref, p_ref, sa_ref, sf_ref, *rest):
    mix_refs = rest[:N_MIX_REFS]
    ffn_refs = rest[N_MIX_REFS:N_MIX_REFS + N_FFN_REFS]
    y_ref, ca_ref, cv_ref, cf_ref = rest[N_MIX_REFS + N_FFN_REFS:]
    sa0, sa1 = sa_ref[:, 0:D_A], sa_ref[:, D_A:2 * D_A]
    sf0, sf1 = sf_ref[:, 0:2 * D_FF], sf_ref[:, 2 * D_FF:4 * D_FF]
    mo, fo = {}, {}
    _run("", M=_mixers(False, x_ref[...], (sa0, sa1), mix_refs, mo))
    _run("", F=_ffn_ple(False, mo["h"], p_ref[...], (sf0, sf1), ffn_refs, fo))
    y_ref[...] = fo["y"]
    ca_ref[:, 0:D_A] = sa1
    ca_ref[:, D_A:2 * D_A] = mo["cx"]
    cv_ref[...] = mo["v"]
    cf_ref[:, 0:2 * D_FF] = sf1
    for off, up in fo["up"]:
        cf_ref[:, 2 * D_FF + off:2 * D_FF + off + up.shape[1]] = up


def _full(shape):
    return pl.BlockSpec(shape, lambda *_: (0,) * len(shape))


def kernel(x_prompt, x_sample, p_prompt, p_sample, state_conv_a, state_conv_ffn, g_mix, w_in, w_conv_a, w_out_a, g_v, w_spatial, b_spatial, w_out_b, w_o, g_ffn, w_up, w_conv_ffn, w_down, g_ple, w_ple_gate, w_ple, g_final):
    depth = g_mix.shape[0]
    assert depth == 1
    batch, seq_len, _ = x_prompt.shape
    dec_batch, dec_seq, _ = x_sample.shape
    assert dec_seq == 1 and seq_len % TM == 0 and TM % CHUNK == 0

    row = lambda a: a.reshape(1, -1)
    mats = ("w_in", "w_out_a", "w_out_b", "w_o", "w_up", "w_down", "w_ple_gate", "w_ple")
    packed = _pack_weights([w_in[0], w_out_a[0], w_out_b[0], w_o[0], w_up[0], w_down[0],
                            w_ple_gate[0], w_ple[0]])
    shared = dict(
        zip(mats, packed),
        g_mix=row(g_mix[0]), w_conv_a=w_conv_a[0], g_v=row(g_v[0]), g_ffn=row(g_ffn[0]),
        w_conv_ffn=w_conv_ffn[0], g_ple=row(g_ple[0]), g_final=row(g_final))
    order = ("g_mix", "w_in", "w_conv_a", "w_out_a", "g_v", "w_sp", "b_sp", "w_out_b", "w_o",
             "g_ffn", "w_up", "w_conv_ffn", "w_down", "g_ple", "w_ple_gate", "w_ple", "g_final")
    assert len(order) == N_MIX_REFS + N_FFN_REFS

    pw = dict(shared, w_sp=w_spatial[0].astype(bf16), b_sp=b_spatial[0].T)
    pw_args = [pw[k] for k in order]
    n_l = seq_len // TM
    n_tiles = batch * n_l

    def mix_tile(i):
        return jnp.minimum(i, n_tiles - 1)

    def ffn_tile(i):
        return jnp.maximum(i - 1, 0)

    y_p, ca_p, cv_p, cf_p = pl.pallas_call(
        functools.partial(_prompt_kernel, n_l),
        grid=(n_tiles + 1,),
        in_specs=[pl.BlockSpec((1, TM, D_MODEL), lambda i: (mix_tile(i) // n_l, mix_tile(i) % n_l, 0)),
                  pl.BlockSpec((1, TM, P_DIM), lambda i: (ffn_tile(i) // n_l, ffn_tile(i) % n_l, 0))]
                 + [_full(a.shape) for a in pw_args],
        out_specs=[pl.BlockSpec((1, TM, D_MODEL), lambda i: (ffn_tile(i) // n_l, ffn_tile(i) % n_l, 0)),
                   pl.BlockSpec((1, CONV_W - 1, D_A), lambda i: (mix_tile(i) // n_l, 0, 0)),
                   pl.BlockSpec((1, CHUNK, D_B), lambda i: (mix_tile(i) // n_l, 0, 0)),
                   pl.BlockSpec((1, CONV_W - 1, 2 * D_FF), lambda i: (ffn_tile(i) // n_l, 0, 0))],
        out_shape=[jax.ShapeDtypeStruct((batch, seq_len, D_MODEL), f32),
                   jax.ShapeDtypeStruct((batch, CONV_W - 1, D_A), f32),
                   jax.ShapeDtypeStruct((batch, CHUNK, D_B), f32),
                   jax.ShapeDtypeStruct((batch, CONV_W - 1, 2 * D_FF), f32)],
        scratch_shapes=[pltpu.VMEM((D_A // LANES, TM + SUBLANES, LANES), f32),
                        pltpu.VMEM((2 * D_FF // LANES, TM + SUBLANES, LANES), f32),
                        pltpu.VMEM((TM, D_MODEL), f32)],
        compiler_params=pltpu.CompilerParams(
            dimension_semantics=("arbitrary",), vmem_limit_bytes=VMEM_LIMIT_BYTES),
        name="layer_prompt",
    )(x_prompt, p_prompt[0], *pw_args)

    sw = dict(shared,
              w_sp=row(jnp.repeat(w_spatial[0, :, 0, 0], DG_B)),
              b_sp=row(jnp.repeat(b_spatial[0, :, 0], DG_B)))
    sw_args = [sw[k] for k in order]
    s_in = [x_sample.reshape(dec_batch, D_MODEL), p_sample[0].reshape(dec_batch, P_DIM),
            state_conv_a[0].reshape(dec_batch, (CONV_W - 1) * D_A),
            state_conv_ffn[0].reshape(dec_batch, (CONV_W - 1) * 2 * D_FF)]
    y_s, ca_s, cv_s, cf_s = pl.pallas_call(
        _sample_kernel,
        in_specs=[_full(a.shape) for a in s_in + sw_args],
        out_specs=[_full((dec_batch, D_MODEL)), _full((dec_batch, (CONV_W - 1) * D_A)),
                   _full((dec_batch, D_B)), _full((dec_batch, (CONV_W - 1) * 2 * D_FF))],
        out_shape=[jax.ShapeDtypeStruct((dec_batch, D_MODEL), f32),
                   jax.ShapeDtypeStruct((dec_batch, (CONV_W - 1) * D_A), f32),
                   jax.ShapeDtypeStruct((dec_batch, D_B), f32),
                   jax.ShapeDtypeStruct((dec_batch, (CONV_W - 1) * 2 * D_FF), f32)],
        compiler_params=pltpu.CompilerParams(vmem_limit_bytes=VMEM_LIMIT_BYTES),
        name="layer_sample",
    )(*s_in, *sw_args)

    return (y_p, y_s.reshape(dec_batch, 1, D_MODEL),
            ca_p[None], ca_s.reshape(1, dec_batch, CONV_W - 1, D_A),
            cv_p[None], cv_s.reshape(1, dec_batch, 1, D_B),
            cf_p[None], cf_s.reshape(1, dec_batch, CONV_W - 1, 2 * D_FF))
```

```python
import functools

import jax
import jax.numpy as jnp
from jax import lax
from jax.experimental import pallas as pl
from jax.experimental.pallas import tpu as pltpu

D_MODEL = 1024
D_A = 1024
D_B = 1024
G_B = 8
DG_B = D_B // G_B
CHUNK = 128
CONV_W = 3
D_FF = 2816
P_DIM = 256
EPS = 1e-6

SUBLANES = 8
LANES = 128
TM = 256
MXU_N = 256
FF_SPLITS = (0, 6 * MXU_N, D_FF)
PACK_STEPS = 16
VMEM_LIMIT_BYTES = 62 * 1024 * 1024

_XA, _BA, _CA, _U, _V, _GA, _GB = (0, D_A, 2 * D_A, 3 * D_A, 3 * D_A + D_B,
                                   3 * D_A + 2 * D_B, 3 * D_A + 2 * D_B + D_MODEL)

bf16 = jnp.bfloat16
f32 = jnp.float32


def _pack_kernel(*refs):
    n = len(refs) // 2
    for w_ref, o_ref in zip(refs[:n], refs[n:]):
        o_ref[...] = pltpu.bitcast(w_ref[...].astype(bf16), jnp.uint32)


def _pack_weights(ws):
    for w in ws:
        assert w.shape[0] % (2 * SUBLANES * PACK_STEPS) == 0
    blocks = [(w.shape[0] // PACK_STEPS, w.shape[1]) for w in ws]
    return pl.pallas_call(
        _pack_kernel,
        grid=(PACK_STEPS,),
        in_specs=[pl.BlockSpec(b, lambda i: (i, 0)) for b in blocks],
        out_specs=[pl.BlockSpec((b[0] // 2, b[1]), lambda i: (i, 0)) for b in blocks],
        out_shape=[jax.ShapeDtypeStruct((w.shape[0] // 2, w.shape[1]), jnp.uint32) for w in ws],
        compiler_params=pltpu.CompilerParams(
            dimension_semantics=("arbitrary",), vmem_limit_bytes=VMEM_LIMIT_BYTES),
        name="pack_weights",
    )(*ws)


def _rms(x, g):
    ms = jnp.mean(x * x, axis=-1, keepdims=True)
    return x * lax.rsqrt(ms + EPS) * g


def _dot(a, b):
    return jnp.dot(a, b, preferred_element_type=f32)


def _wdot(a, w_ref, r0=0, r1=None, c0=0, c1=None):
    r1 = 2 * w_ref.shape[0] if r1 is None else r1
    c1 = w_ref.shape[1] if c1 is None else c1
    w = pltpu.bitcast(w_ref[r0 // 2:r1 // 2, c0:c1], bf16)
    return _dot(a, w)


def _conv_seq(x, w, slab, col0, first):
    t, c = x.shape
    outs = []
    for j in range(c // LANES):
        jj = col0 // LANES + j
        xj = x[:, j * LANES:(j + 1) * LANES]
        wj = w[:, j * LANES:(j + 1) * LANES]
        slab[jj, 0:SUBLANES, :] = jnp.where(first, 0.0, slab[jj, t:t + SUBLANES, :])
        slab[jj, SUBLANES:SUBLANES + t, :] = xj
        outs.append(wj[0:1] * slab[jj, SUBLANES - 2:SUBLANES - 2 + t, :]
                    + wj[1:2] * slab[jj, SUBLANES - 1:SUBLANES - 1 + t, :] + wj[2:3] * xj)
    return jnp.concatenate(outs, axis=1)


def _mixers(seq, x, prev_a, refs, out):
    (g_mix, w_in, w_conv_a, w_out_a, g_v, w_sp, b_sp, w_out_b, w_o) = refs
    t = x.shape[0]
    xn = _rms(x, g_mix[...]).astype(bf16)
    yield

    v = jax.nn.gelu(_wdot(xn, w_in, c0=_V, c1=_V + D_B))
    u = jax.nn.gelu(_wdot(xn, w_in, c0=_U, c1=_U + D_B))
    v = _rms(v, g_v[...])
    out["v"] = v
    yield

    if seq:
        vb = v.astype(bf16)
        ti = lax.broadcasted_iota(jnp.int32, (CHUNK, CHUNK), 0)
        si = lax.broadcasted_iota(jnp.int32, (CHUNK, CHUNK), 1)
        causal = si <= ti
        bias = b_sp[...]
        n_c = t // CHUNK
        cols = []
        for g in range(G_B):
            wg = jnp.where(causal, w_sp[g], jnp.zeros((), bf16))
            bg = bias[:, g:g + 1]
            vg = jnp.concatenate([vb[c * CHUNK:(c + 1) * CHUNK, g * DG_B:(g + 1) * DG_B]
                                  for c in range(n_c)], axis=1)
            sg = _dot(wg, vg)
            cols.append(jnp.concatenate([sg[:, c * DG_B:(c + 1) * DG_B] + bg for c in range(n_c)],
                                        axis=0))
        s = jnp.concatenate(cols, axis=1)
    else:
        s = w_sp[...] * v + b_sp[...]
    us = (u * s).astype(bf16)

    xa = _wdot(xn, w_in, c0=_XA, c1=_XA + D_A)
    ca = _wdot(xn, w_in, c0=_CA, c1=_CA + D_A)
    cx = ca * xa
    out["cx"] = cx
    wca = w_conv_a[...]
    if seq:
        conv_a = _conv_seq(cx, wca, prev_a[0], 0, prev_a[1])
    else:
        conv_a = wca[0:1] * prev_a[0] + wca[1:2] * prev_a[1] + wca[2:3] * cx
    ba = _wdot(xn, w_in, c0=_BA, c1=_BA + D_A)
    yield

    y_a = _wdot((ba * conv_a).astype(bf16), w_out_a)
    y_b = _wdot(us, w_out_b)
    ga = _wdot(xn, w_in, c0=_GA, c1=_GA + D_MODEL)
    gb = _wdot(xn, w_in, c0=_GB, c1=_GB + D_MODEL)
    m = jax.nn.sigmoid(ga) * y_a + jax.nn.sigmoid(gb) * y_b
    yield

    out["h"] = x + _wdot(m.astype(bf16), w_o)
    yield


def _ffn_ple(seq, h, p, prev_f, refs, out):
    (g_ffn, w_up, w_conv_ffn, w_down, g_ple, w_ple_gate, w_ple, g_final) = refs
    pe = _wdot(p.astype(bf16), w_ple)
    xn2 = _rms(h, g_ffn[...]).astype(bf16)
    yield

    acc = None
    out["up"] = []
    for c0, c1 in zip(FF_SPLITS[:-1], FF_SPLITS[1:]):
        n = c1 - c0
        halves = []
        for off in (c0, D_FF + c0):
            up = _wdot(xn2, w_up, c0=off, c1=off + n)
            wcf = w_conv_ffn[:, off:off + n]
            if seq:
                conv = _conv_seq(up, wcf, prev_f[0], off, prev_f[1])
            else:
                conv = (wcf[0:1] * prev_f[0][:, off:off + n]
                        + wcf[1:2] * prev_f[1][:, off:off + n] + wcf[2:3] * up)
            out["up"].append((off, up))
            halves.append(conv)
        hid = (jax.nn.gelu(halves[0]) * halves[1]).astype(bf16)
        yield
        part = _wdot(hid, w_down, r0=c0, r1=c1)
        acc = part if acc is None else acc + part
        yield
    h = h + acc

    gate = jax.nn.sigmoid(_wdot(_rms(h, g_ple[...]).astype(bf16), w_ple_gate))
    yield
    h = h + gate * pe
    out["y"] = _rms(h, g_final[...])
    yield


def _run(schedule, **gens):
    for name in schedule:
        next(gens[name])
    for g in gens.values():
        for _ in g:
            pass


PIPELINE_ORDER = "FMFMFFMFMFMF"

N_MIX_REFS = 9
N_FFN_REFS = 8


def _prompt_kernel(tiles_per_seq, x_ref, p_ref, *rest):
    mix_refs = rest[:N_MIX_REFS]
    ffn_refs = rest[N_MIX_REFS:N_MIX_REFS + N_FFN_REFS]
    y_ref, ca_ref, cv_ref, cf_ref, slab_a, slab_f, h_scr = rest[N_MIX_REFS + N_FFN_REFS:]
    i = pl.program_id(0)
    last = pl.num_programs(0) - 1

    @pl.when(i == 0)
    def _():
        slab_a[:, TM:TM + SUBLANES, :] = jnp.zeros((slab_a.shape[0], SUBLANES, LANES), f32)
        slab_f[:, TM:TM + SUBLANES, :] = jnp.zeros((slab_f.shape[0], SUBLANES, LANES), f32)
        h_scr[...] = jnp.zeros_like(h_scr)

    def ffn_of_previous_tile(out):
        first = (jnp.maximum(i - 1, 0) % tiles_per_seq) == 0
        return _ffn_ple(True, h_scr[...], p_ref[0], (slab_f, first), ffn_refs, out)

    def store_ffn(out):
        y_ref[0] = out["y"]
        for off, up in out["up"]:
            cf_ref[0, :, off:off + up.shape[1]] = up[TM - (CONV_W - 1):TM]

    @pl.when(i < last)
    def _():
        first = (i % tiles_per_seq) == 0
        mo, fo = {}, {}
        _run(PIPELINE_ORDER, M=_mixers(True, x_ref[0], (slab_a, first), mix_refs, mo),
             F=ffn_of_previous_tile(fo))
        store_ffn(fo)
        h_scr[...] = mo["h"]
        cx, v = mo["cx"], mo["v"]
        ca_ref[0] = cx[TM - (CONV_W - 1):TM]
        cv_ref[0] = v[TM - CHUNK:TM]

    @pl.when(i == last)
    def _():
        fo = {}
        _run("", F=ffn_of_previous_tile(fo))
        store_ffn(fo)


def _sample_kernel(x_ref, p_ref, sa_ref, sf_ref, *rest):
    mix_refs = rest[:N_MIX_REFS]
    ffn_refs = rest[N_MIX_REFS:N_MIX_REFS + N_FFN_REFS]
    y_ref, ca_ref, cv_ref, cf_ref = rest[N_MIX_REFS + N_FFN_REFS:]
    sa0, sa1 = sa_ref[:, 0, :], sa_ref[:, 1, :]
    sf0, sf1 = sf_ref[:, 0, :], sf_ref[:, 1, :]
    mo, fo = {}, {}
    _run("", M=_mixers(False, x_ref[:, 0, :], (sa0, sa1), mix_refs, mo))
    _run("", F=_ffn_ple(False, mo["h"], p_ref[:, 0, :], (sf0, sf1), ffn_refs, fo))
    y_ref[:, 0, :] = fo["y"]
    ca_ref[:, 0, :] = sa1
    ca_ref[:, 1, :] = mo["cx"]
    cv_ref[:, 0, :] = mo["v"]
    cf_ref[:, 0, :] = sf1
    for off, up in fo["up"]:
        cf_ref[:, 1, off:off + up.shape[1]] = up


def _full(shape):
    return pl.BlockSpec(shape, lambda *_: (0,) * len(shape))


def kernel(x_prompt, x_sample, p_prompt, p_sample, state_conv_a, state_conv_ffn, g_mix, w_in, w_conv_a, w_out_a, g_v, w_spatial, b_spatial, w_out_b, w_o, g_ffn, w_up, w_conv_ffn, w_down, g_ple, w_ple_gate, w_ple, g_final):
    depth = g_mix.shape[0]
    assert depth == 1
    batch, seq_len, _ = x_prompt.shape
    dec_batch, dec_seq, _ = x_sample.shape
    assert dec_seq == 1 and seq_len % TM == 0 and TM % CHUNK == 0

    row = lambda a: a.reshape(1, -1)
    mats = ("w_in", "w_out_a", "w_out_b", "w_o", "w_up", "w_down", "w_ple_gate", "w_ple")
    packed = _pack_weights([w_in[0], w_out_a[0], w_out_b[0], w_o[0], w_up[0], w_down[0],
                            w_ple_gate[0], w_ple[0]])
    shared = dict(
        zip(mats, packed),
        g_mix=row(g_mix[0]), w_conv_a=w_conv_a[0], g_v=row(g_v[0]), g_ffn=row(g_ffn[0]),
        w_conv_ffn=w_conv_ffn[0], g_ple=row(g_ple[0]), g_final=row(g_final))
    order = ("g_mix", "w_in", "w_conv_a", "w_out_a", "g_v", "w_sp", "b_sp", "w_out_b", "w_o",
             "g_ffn", "w_up", "w_conv_ffn", "w_down", "g_ple", "w_ple_gate", "w_ple", "g_final")
    assert len(order) == N_MIX_REFS + N_FFN_REFS

    pw = dict(shared, w_sp=w_spatial[0].astype(bf16), b_sp=b_spatial[0].T)
    pw_args = [pw[k] for k in order]
    n_l = seq_len // TM
    n_tiles = batch * n_l

    def mix_tile(i):
        return jnp.minimum(i, n_tiles - 1)

    def ffn_tile(i):
        return jnp.maximum(i - 1, 0)

    y_p, ca_p, cv_p, cf_p = pl.pallas_call(
        functools.partial(_prompt_kernel, n_l),
        grid=(n_tiles + 1,),
        in_specs=[pl.BlockSpec((1, TM, D_MODEL), lambda i: (mix_tile(i) // n_l, mix_tile(i) % n_l, 0)),
                  pl.BlockSpec((1, TM, P_DIM), lambda i: (ffn_tile(i) // n_l, ffn_tile(i) % n_l, 0))]
                 + [_full(a.shape) for a in pw_args],
        out_specs=[pl.BlockSpec((1, TM, D_MODEL), lambda i: (ffn_tile(i) // n_l, ffn_tile(i) % n_l, 0)),
                   pl.BlockSpec((1, CONV_W - 1, D_A), lambda i: (mix_tile(i) // n_l, 0, 0)),
                   pl.BlockSpec((1, CHUNK, D_B), lambda i: (mix_tile(i) // n_l, 0, 0)),
                   pl.BlockSpec((1, CONV_W - 1, 2 * D_FF), lambda i: (ffn_tile(i) // n_l, 0, 0))],
        out_shape=[jax.ShapeDtypeStruct((batch, seq_len, D_MODEL), f32),
                   jax.ShapeDtypeStruct((batch, CONV_W - 1, D_A), f32),
                   jax.ShapeDtypeStruct((batch, CHUNK, D_B), f32),
                   jax.ShapeDtypeStruct((batch, CONV_W - 1, 2 * D_FF), f32)],
        scratch_shapes=[pltpu.VMEM((D_A // LANES, TM + SUBLANES, LANES), f32),
                        pltpu.VMEM((2 * D_FF // LANES, TM + SUBLANES, LANES), f32),
                        pltpu.VMEM((TM, D_MODEL), f32)],
        compiler_params=pltpu.CompilerParams(
            dimension_semantics=("arbitrary",), vmem_limit_bytes=VMEM_LIMIT_BYTES),
        name="layer_prompt",
    )(x_prompt, p_prompt[0], *pw_args)

    sw = dict(shared,
              w_sp=row(jnp.repeat(w_spatial[0, :, 0, 0], DG_B)),
              b_sp=row(jnp.repeat(b_spatial[0, :, 0], DG_B)))
    sw_args = [sw[k] for k in order]
    s_in = [x_sample, p_sample[0], state_conv_a[0], state_conv_ffn[0]]
    s_out = [(dec_batch, 1, D_MODEL), (dec_batch, CONV_W - 1, D_A), (dec_batch, 1, D_B),
             (dec_batch, CONV_W - 1, 2 * D_FF)]
    y_s, ca_s, cv_s, cf_s = pl.pallas_call(
        _sample_kernel,
        in_specs=[_full(a.shape) for a in s_in + sw_args],
        out_specs=[_full(s) for s in s_out],
        out_shape=[jax.ShapeDtypeStruct(s, f32) for s in s_out],
        compiler_params=pltpu.CompilerParams(vmem_limit_bytes=VMEM_LIMIT_BYTES),
        name="layer_sample",
    )(*s_in, *sw_args)

    return (y_p, y_s, ca_p[None], ca_s[None], cv_p[None], cv_s[None], cf_p[None], cf_s[None])
```

```python
import functools

import jax
import jax.numpy as jnp
from jax import lax
from jax.experimental import pallas as pl
from jax.experimental.pallas import tpu as pltpu

D_MODEL = 1024
D_A = 1024
D_B = 1024
G_B = 8
DG_B = D_B // G_B
CHUNK = 128
CONV_W = 3
D_FF = 2816
P_DIM = 256
EPS = 1e-6

SUBLANES = 8
LANES = 128
TM = 256
MXU_N = 256
FF_SPLITS = (0, 6 * MXU_N, D_FF)
VMEM_LIMIT_BYTES = 62 * 1024 * 1024

_XA, _BA, _CA, _U, _V, _GA, _GB = (0, D_A, 2 * D_A, 3 * D_A, 3 * D_A + D_B,
                                   3 * D_A + 2 * D_B, 3 * D_A + 2 * D_B + D_MODEL)

bf16 = jnp.bfloat16
f32 = jnp.float32


IN_FLIGHT = 2
OUT_SLOTS = 2
_CHUNKING = {"w_in": ("wide", 1, 512), "w_up": ("wide", 1, 512),
             "w_out_a": ("square", 0, 512), "w_out_b": ("square", 0, 512), "w_o": ("square", 0, 512),
             "w_ple_gate": ("square", 0, 512), "w_down": ("down", 0, 768), "w_ple": ("ple", 0, 256)}
_CLASS_SHAPE = {"wide": (D_MODEL, 512), "square": (512, D_MODEL), "down": (768, D_MODEL),
                "ple": (P_DIM, D_MODEL)}
_CLASSES = tuple(_CLASS_SHAPE)


def _sample_requests():
    full = lambda name, k: (name, 0, k, 0, D_MODEL)
    w_in = lambda c0: ("w_in", 0, D_MODEL, c0, c0 + D_MODEL)
    reqs = [w_in(_V), w_in(_U), w_in(_XA), w_in(_CA), w_in(_BA), full("w_out_a", D_A),
            full("w_out_b", D_B), w_in(_GA), w_in(_GB), full("w_o", D_MODEL), full("w_ple", P_DIM)]
    for c0, c1 in zip(FF_SPLITS[:-1], FF_SPLITS[1:]):
        reqs += [("w_up", 0, D_MODEL, c0, c1), ("w_up", 0, D_MODEL, D_FF + c0, D_FF + c1),
                 ("w_down", c0, c1, 0, D_MODEL)]
    reqs.append(full("w_ple_gate", D_MODEL))
    return reqs


def _sample_chunks():
    per_class = dict.fromkeys(_CLASSES, 0)
    plan = []
    for name, r0, r1, c0, c1 in _sample_requests():
        cls, axis, size = _CHUNKING[name]
        lo, hi = ((r0, r1), (c0, c1))[axis]
        chunks = []
        for s in range(lo, hi, size):
            e = min(s + size, hi)
            rows, cols = ((s, e), (c0, c1)) if axis == 0 else ((r0, r1), (s, e))
            chunks.append((name, cls, per_class[cls], rows, cols))
            per_class[cls] += 1
        plan.append(chunks)
    return plan, per_class


def _ring_depth(n_chunks, depth):
    return min(n_chunks, depth)


class _WeightStream:
    def __init__(self, w_hbm, packed_hbm, stage, pstage, in_sems, out_sems):
        self._w, self._packed = w_hbm, packed_hbm
        self._stage, self._pstage, self._in_sems, self._out_sems = stage, pstage, in_sems, out_sems
        self._plan, per_class = _sample_chunks()
        self._flat = [c for chunks in self._plan for c in chunks]
        self._in_depth = {c: _ring_depth(n, IN_FLIGHT + 1) for c, n in per_class.items()}
        self._out_depth = {c: _ring_depth(n, OUT_SLOTS) for c, n in per_class.items()}
        self._reads = {}
        self._writes = {}
        self._next_request = 0
        self._next_chunk = 0
        for j in range(IN_FLIGHT):
            self._start_read(j)

    def _start_read(self, j):
        if j >= len(self._flat):
            return
        name, cls, k, (r0, r1), (c0, c1) = self._flat[j]
        slot = k % self._in_depth[cls]
        copy = pltpu.make_async_copy(self._w[name].at[r0:r1, c0:c1],
                                     self._stage[cls].at[slot, 0:r1 - r0, 0:c1 - c0],
                                     self._in_sems[cls].at[slot])
        copy.start()
        self._reads[j] = copy

    def _take(self):
        j = self._next_chunk
        self._next_chunk += 1
        name, cls, k, (r0, r1), (c0, c1) = self._flat[j]
        nr, nc = r1 - r0, c1 - c0
        self._reads.pop(j).wait()
        w = self._stage[cls][k % self._in_depth[cls], 0:nr, 0:nc].astype(bf16)
        self._start_read(j + IN_FLIGHT)
        pslot = k % self._out_depth[cls]
        if (cls, pslot) in self._writes:
            self._writes.pop((cls, pslot)).wait()
        self._pstage[cls][pslot, 0:nr // 2, 0:nc] = pltpu.bitcast(w, jnp.uint32)
        copy = pltpu.make_async_copy(self._pstage[cls].at[pslot, 0:nr // 2, 0:nc],
                                     self._packed[name].at[r0 // 2:r1 // 2, c0:c1],
                                     self._out_sems[cls].at[pslot])
        copy.start()
        self._writes[(cls, pslot)] = copy
        return w

    def dot(self, name, a, r0, r1, c0, c1):
        chunks = self._plan[self._next_request]
        assert _sample_requests()[self._next_request] == (name, r0, r1, c0, c1)
        self._next_request += 1
        if _CHUNKING[name][1] == 1:
            return jnp.concatenate([_dot(a, self._take()) for _ in chunks], axis=1)
        acc = None
        for _, _, _, (s, e), _ in chunks:
            part = _dot(a[:, s - r0:e - r0], self._take())
            acc = part if acc is None else acc + part
        return acc

    def finish(self):
        assert self._next_chunk == len(self._flat) and not self._reads
        for copy in self._writes.values():
            copy.wait()
        self._writes = {}


class _StreamedWeight:
    def __init__(self, stream, name, shape):
        self.stream, self.name, self.shape = stream, name, shape


def _rms(x, g):
    ms = jnp.mean(x * x, axis=-1, keepdims=True)
    return x * lax.rsqrt(ms + EPS) * g


def _dot(a, b):
    return jnp.dot(a, b, preferred_element_type=f32)


def _wdot(a, w_ref, r0=0, r1=None, c0=0, c1=None):
    if isinstance(w_ref, _StreamedWeight):
        r1 = w_ref.shape[0] if r1 is None else r1
        c1 = w_ref.shape[1] if c1 is None else c1
        return w_ref.stream.dot(w_ref.name, a, r0, r1, c0, c1)
    r1 = 2 * w_ref.shape[0] if r1 is None else r1
    c1 = w_ref.shape[1] if c1 is None else c1
    w = pltpu.bitcast(w_ref[r0 // 2:r1 // 2, c0:c1], bf16)
    return _dot(a, w)


def _conv_seq(x, w, slab, col0, first):
    t, c = x.shape
    outs = []
    for j in range(c // LANES):
        jj = col0 // LANES + j
        xj = x[:, j * LANES:(j + 1) * LANES]
        wj = w[:, j * LANES:(j + 1) * LANES]
        slab[jj, 0:SUBLANES, :] = jnp.where(first, 0.0, slab[jj, t:t + SUBLANES, :])
        slab[jj, SUBLANES:SUBLANES + t, :] = xj
        outs.append(wj[0:1] * slab[jj, SUBLANES - 2:SUBLANES - 2 + t, :]
                    + wj[1:2] * slab[jj, SUBLANES - 1:SUBLANES - 1 + t, :] + wj[2:3] * xj)
    return jnp.concatenate(outs, axis=1)


def _mixers(seq, x, prev_a, refs, out):
    (g_mix, w_in, w_conv_a, w_out_a, g_v, w_sp, b_sp, w_out_b, w_o) = refs
    t = x.shape[0]
    xn = _rms(x, g_mix[...]).astype(bf16)
    yield

    v = jax.nn.gelu(_wdot(xn, w_in, c0=_V, c1=_V + D_B))
    u = jax.nn.gelu(_wdot(xn, w_in, c0=_U, c1=_U + D_B))
    v = _rms(v, g_v[...])
    out["v"] = v
    yield

    if seq:
        vb = v.astype(bf16)
        ti = lax.broadcasted_iota(jnp.int32, (CHUNK, CHUNK), 0)
        si = lax.broadcasted_iota(jnp.int32, (CHUNK, CHUNK), 1)
        causal = si <= ti
        bias = b_sp[...]
        n_c = t // CHUNK
        cols = []
        for g in range(G_B):
            wg = jnp.where(causal, w_sp[g], jnp.zeros((), bf16))
            bg = bias[:, g:g + 1]
            vg = jnp.concatenate([vb[c * CHUNK:(c + 1) * CHUNK, g * DG_B:(g + 1) * DG_B]
                                  for c in range(n_c)], axis=1)
            sg = _dot(wg, vg)
            cols.append(jnp.concatenate([sg[:, c * DG_B:(c + 1) * DG_B] + bg for c in range(n_c)],
                                        axis=0))
        s = jnp.concatenate(cols, axis=1)
    else:
        s = w_sp[...] * v + b_sp[...]
    us = (u * s).astype(bf16)

    xa = _wdot(xn, w_in, c0=_XA, c1=_XA + D_A)
    ca = _wdot(xn, w_in, c0=_CA, c1=_CA + D_A)
    cx = ca * xa
    out["cx"] = cx
    wca = w_conv_a[...]
    if seq:
        conv_a = _conv_seq(cx, wca, prev_a[0], 0, prev_a[1])
    else:
        conv_a = wca[0:1] * prev_a[0] + wca[1:2] * prev_a[1] + wca[2:3] * cx
    ba = _wdot(xn, w_in, c0=_BA, c1=_BA + D_A)
    yield

    y_a = _wdot((ba * conv_a).astype(bf16), w_out_a)
    y_b = _wdot(us, w_out_b)
    ga = _wdot(xn, w_in, c0=_GA, c1=_GA + D_MODEL)
    gb = _wdot(xn, w_in, c0=_GB, c1=_GB + D_MODEL)
    m = jax.nn.sigmoid(ga) * y_a + jax.nn.sigmoid(gb) * y_b
    yield

    out["h"] = x + _wdot(m.astype(bf16), w_o)
    yield


def _ffn_ple(seq, h, p, prev_f, refs, out):
    (g_ffn, w_up, w_conv_ffn, w_down, g_ple, w_ple_gate, w_ple, g_final) = refs
    pe = _wdot(p.astype(bf16), w_ple)
    xn2 = _rms(h, g_ffn[...]).astype(bf16)
    yield

    acc = None
    out["up"] = []
    for c0, c1 in zip(FF_SPLITS[:-1], FF_SPLITS[1:]):
        n = c1 - c0
        halves = []
        for off in (c0, D_FF + c0):
            up = _wdot(xn2, w_up, c0=off, c1=off + n)
            wcf = w_conv_ffn[:, off:off + n]
            if seq:
                conv = _conv_seq(up, wcf, prev_f[0], off, prev_f[1])
            else:
                conv = (wcf[0:1] * prev_f[0][:, off:off + n]
                        + wcf[1:2] * prev_f[1][:, off:off + n] + wcf[2:3] * up)
            out["up"].append((off, up))
            halves.append(conv)
        hid = (jax.nn.gelu(halves[0]) * halves[1]).astype(bf16)
        yield
        part = _wdot(hid, w_down, r0=c0, r1=c1)
        acc = part if acc is None else acc + part
        yield
    h = h + acc

    gate = jax.nn.sigmoid(_wdot(_rms(h, g_ple[...]).astype(bf16), w_ple_gate))
    yield
    h = h + gate * pe
    out["y"] = _rms(h, g_final[...])
    yield


def _run(schedule, **gens):
    for name in schedule:
        next(gens[name])
    for g in gens.values():
        for _ in g:
            pass


PIPELINE_ORDER = "FMFMFFMFMFMF"

WEIGHT_ORDER = ("g_mix", "w_in", "w_conv_a", "w_out_a", "g_v", "w_sp", "b_sp", "w_out_b", "w_o",
                "g_ffn", "w_up", "w_conv_ffn", "w_down", "g_ple", "w_ple_gate", "w_ple", "g_final")
N_MIX_REFS = 9
N_FFN_REFS = len(WEIGHT_ORDER) - N_MIX_REFS
MATMUL_WEIGHTS = tuple(_CHUNKING)


def _prompt_kernel(tiles_per_seq, x_ref, p_ref, *rest):
    mix_refs = rest[:N_MIX_REFS]
    ffn_refs = rest[N_MIX_REFS:N_MIX_REFS + N_FFN_REFS]
    y_ref, ca_ref, cv_ref, cf_ref, slab_a, slab_f, h_scr = rest[N_MIX_REFS + N_FFN_REFS:]
    i = pl.program_id(0)
    last = pl.num_programs(0) - 1

    @pl.when(i == 0)
    def _():
        slab_a[:, TM:TM + SUBLANES, :] = jnp.zeros((slab_a.shape[0], SUBLANES, LANES), f32)
        slab_f[:, TM:TM + SUBLANES, :] = jnp.zeros((slab_f.shape[0], SUBLANES, LANES), f32)
        h_scr[...] = jnp.zeros_like(h_scr)

    def ffn_of_previous_tile(out):
        first = (jnp.maximum(i - 1, 0) % tiles_per_seq) == 0
        return _ffn_ple(True, h_scr[...], p_ref[0], (slab_f, first), ffn_refs, out)

    def store_ffn(out):
        y_ref[0] = out["y"]
        for off, up in out["up"]:
            cf_ref[0, :, off:off + up.shape[1]] = up[TM - (CONV_W - 1):TM]

    @pl.when(i < last)
    def _():
        first = (i % tiles_per_seq) == 0
        mo, fo = {}, {}
        _run(PIPELINE_ORDER, M=_mixers(True, x_ref[0], (slab_a, first), mix_refs, mo),
             F=ffn_of_previous_tile(fo))
        store_ffn(fo)
        h_scr[...] = mo["h"]
        cx, v = mo["cx"], mo["v"]
        ca_ref[0] = cx[TM - (CONV_W - 1):TM]
        cv_ref[0] = v[TM - CHUNK:TM]

    @pl.when(i == last)
    def _():
        fo = {}
        _run("", F=ffn_of_previous_tile(fo))
        store_ffn(fo)


def _sample_kernel(x_ref, p_ref, sa_ref, sf_ref, *rest):
    n_w, n_m, n_c = N_MIX_REFS + N_FFN_REFS, len(MATMUL_WEIGHTS), len(_CLASSES)
    weights = list(rest[:n_w])
    y_ref, ca_ref, cv_ref, cf_ref = rest[n_w:n_w + 4]
    packed = dict(zip(MATMUL_WEIGHTS, rest[n_w + 4:n_w + 4 + n_m]))
    scratch = rest[n_w + 4 + n_m:]
    stage, pstage, in_sems, out_sems = (dict(zip(_CLASSES, scratch[k * n_c:(k + 1) * n_c]))
                                        for k in range(4))
    w_hbm = {name: weights[WEIGHT_ORDER.index(name)] for name in MATMUL_WEIGHTS}
    stream = _WeightStream(w_hbm, packed, stage, pstage, in_sems, out_sems)
    for name in MATMUL_WEIGHTS:
        weights[WEIGHT_ORDER.index(name)] = _StreamedWeight(stream, name, w_hbm[name].shape)
    mix_refs = weights[:N_MIX_REFS]
    ffn_refs = weights[N_MIX_REFS:]
    sa0, sa1 = sa_ref[:, 0, :], sa_ref[:, 1, :]
    sf0, sf1 = sf_ref[:, 0, :], sf_ref[:, 1, :]
    mo, fo = {}, {}
    _run("", M=_mixers(False, x_ref[:, 0, :], (sa0, sa1), mix_refs, mo))
    _run("", F=_ffn_ple(False, mo["h"], p_ref[:, 0, :], (sf0, sf1), ffn_refs, fo))
    y_ref[:, 0, :] = fo["y"]
    ca_ref[:, 0, :] = sa1
    ca_ref[:, 1, :] = mo["cx"]
    cv_ref[:, 0, :] = mo["v"]
    cf_ref[:, 0, :] = sf1
    for off, up in fo["up"]:
        cf_ref[:, 1, off:off + up.shape[1]] = up
    stream.finish()


def _full(shape):
    return pl.BlockSpec(shape, lambda *_: (0,) * len(shape))


def kernel(x_prompt, x_sample, p_prompt, p_sample, state_conv_a, state_conv_ffn, g_mix, w_in, w_conv_a, w_out_a, g_v, w_spatial, b_spatial, w_out_b, w_o, g_ffn, w_up, w_conv_ffn, w_down, g_ple, w_ple_gate, w_ple, g_final):
    depth = g_mix.shape[0]
    assert depth == 1
    batch, seq_len, _ = x_prompt.shape
    dec_batch, dec_seq, _ = x_sample.shape
    assert dec_seq == 1 and seq_len % TM == 0 and TM % CHUNK == 0

    row = lambda a: a.reshape(1, -1)
    small = dict(g_mix=row(g_mix[0]), w_conv_a=w_conv_a[0], g_v=row(g_v[0]), g_ffn=row(g_ffn[0]),
                 w_conv_ffn=w_conv_ffn[0], g_ple=row(g_ple[0]), g_final=row(g_final))
    mats = dict(w_in=w_in[0], w_out_a=w_out_a[0], w_out_b=w_out_b[0], w_o=w_o[0], w_up=w_up[0],
                w_down=w_down[0], w_ple_gate=w_ple_gate[0], w_ple=w_ple[0])

    sw = dict(small, **mats,
              w_sp=row(jnp.repeat(w_spatial[0, :, 0, 0], DG_B)),
              b_sp=row(jnp.repeat(b_spatial[0, :, 0], DG_B)))
    sw_args = [sw[k] for k in WEIGHT_ORDER]
    s_in = [x_sample, p_sample[0], state_conv_a[0], state_conv_ffn[0]]
    s_out = [(dec_batch, 1, D_MODEL), (dec_batch, CONV_W - 1, D_A), (dec_batch, 1, D_B),
             (dec_batch, CONV_W - 1, 2 * D_FF)]
    packed_shapes = [(mats[k].shape[0] // 2, mats[k].shape[1]) for k in MATMUL_WEIGHTS]
    _, per_class = _sample_chunks()
    in_depth = {c: _ring_depth(per_class[c], IN_FLIGHT + 1) for c in _CLASSES}
    out_depth = {c: _ring_depth(per_class[c], OUT_SLOTS) for c in _CLASSES}
    hbm = pl.BlockSpec(memory_space=pl.ANY)
    outs = pl.pallas_call(
        _sample_kernel,
        in_specs=[_full(a.shape) for a in s_in]
                 + [hbm if k in MATMUL_WEIGHTS else _full(sw[k].shape) for k in WEIGHT_ORDER],
        out_specs=[_full(s) for s in s_out] + [hbm] * len(MATMUL_WEIGHTS),
        out_shape=[jax.ShapeDtypeStruct(s, f32) for s in s_out]
                  + [jax.ShapeDtypeStruct(s, jnp.uint32) for s in packed_shapes],
        scratch_shapes=[pltpu.VMEM((in_depth[c],) + _CLASS_SHAPE[c], f32) for c in _CLASSES]
                       + [pltpu.VMEM((out_depth[c], _CLASS_SHAPE[c][0] // 2, _CLASS_SHAPE[c][1]),
                                     jnp.uint32) for c in _CLASSES]
                       + [pltpu.SemaphoreType.DMA((in_depth[c],)) for c in _CLASSES]
                       + [pltpu.SemaphoreType.DMA((out_depth[c],)) for c in _CLASSES],
        compiler_params=pltpu.CompilerParams(vmem_limit_bytes=VMEM_LIMIT_BYTES),
        name="layer_sample",
    )(*s_in, *sw_args)
    y_s, ca_s, cv_s, cf_s = outs[:4]
    packed = dict(zip(MATMUL_WEIGHTS, outs[4:]))

    pw = dict(small, **packed, w_sp=w_spatial[0].astype(bf16), b_sp=b_spatial[0].T)
    pw_args = [pw[k] for k in WEIGHT_ORDER]
    n_l = seq_len // TM
    n_tiles = batch * n_l

    def mix_tile(i):
        return jnp.minimum(i, n_tiles - 1)

    def ffn_tile(i):
        return jnp.maximum(i - 1, 0)

    y_p, ca_p, cv_p, cf_p = pl.pallas_call(
        functools.partial(_prompt_kernel, n_l),
        grid=(n_tiles + 1,),
        in_specs=[pl.BlockSpec((1, TM, D_MODEL), lambda i: (mix_tile(i) // n_l, mix_tile(i) % n_l, 0)),
                  pl.BlockSpec((1, TM, P_DIM), lambda i: (ffn_tile(i) // n_l, ffn_tile(i) % n_l, 0))]
                 + [_full(a.shape) for a in pw_args],
        out_specs=[pl.BlockSpec((1, TM, D_MODEL), lambda i: (ffn_tile(i) // n_l, ffn_tile(i) % n_l, 0)),
                   pl.BlockSpec((1, CONV_W - 1, D_A), lambda i: (mix_tile(i) // n_l, 0, 0)),
                   pl.BlockSpec((1, CHUNK, D_B), lambda i: (mix_tile(i) // n_l, 0, 0)),
                   pl.BlockSpec((1, CONV_W - 1, 2 * D_FF), lambda i: (ffn_tile(i) // n_l, 0, 0))],
        out_shape=[jax.ShapeDtypeStruct((batch, seq_len, D_MODEL), f32),
                   jax.ShapeDtypeStruct((batch, CONV_W - 1, D_A), f32),
                   jax.ShapeDtypeStruct((batch, CHUNK, D_B), f32),
                   jax.ShapeDtypeStruct((batch, CONV_W - 1, 2 * D_FF), f32)],
        scratch_shapes=[pltpu.VMEM((D_A // LANES, TM + SUBLANES, LANES), f32),
                        pltpu.VMEM((2 * D_FF // LANES, TM + SUBLANES, LANES), f32),
                        pltpu.VMEM((TM, D_MODEL), f32)],
        compiler_params=pltpu.CompilerParams(
            dimension_semantics=("arbitrary",), vmem_limit_bytes=VMEM_LIMIT_BYTES),
        name="layer_prompt",
    )(x_prompt, p_prompt[0], *pw_args)

    return (y_p, y_s, ca_p[None], ca_s[None], cv_p[None], cv_s[None], cf_p[None], cf_s[None])
```

```python
import functools

import jax
import jax.numpy as jnp
from jax import lax
from jax.experimental import pallas as pl
from jax.experimental.pallas import tpu as pltpu

D_MODEL = 1024
D_A = 1024
D_B = 1024
G_B = 8
DG_B = D_B // G_B
CHUNK = 128
CONV_W = 3
D_FF = 2816
P_DIM = 256
EPS = 1e-6

SUBLANES = 8
LANES = 128
TM = 256
MXU_N = 256
FF_SPLITS = (0, 6 * MXU_N, D_FF)
VMEM_LIMIT_BYTES = 62 * 1024 * 1024

_XA, _BA, _CA, _U, _V, _GA, _GB = (0, D_A, 2 * D_A, 3 * D_A, 3 * D_A + D_B,
                                   3 * D_A + 2 * D_B, 3 * D_A + 2 * D_B + D_MODEL)

bf16 = jnp.bfloat16
f32 = jnp.float32


IN_FLIGHT = 4
OUT_SLOTS = 2
_CHUNKING = {"w_in": ("wide", 1, 512), "w_up": ("wide", 1, 512),
             "w_out_a": ("tall", 0, 512), "w_out_b": ("tall", 0, 512), "w_o": ("tall", 0, 512),
             "w_ple_gate": ("tall", 0, 512), "w_down": ("tall", 0, 512), "w_ple": ("ple", 0, 256)}
_CLASS_SHAPE = {"wide": (D_MODEL, 512), "tall": (512, D_MODEL), "ple": (P_DIM, D_MODEL)}
_CLASSES = tuple(_CLASS_SHAPE)


def _sample_requests():
    full = lambda name, k: (name, 0, k, 0, D_MODEL)
    w_in = lambda c0: ("w_in", 0, D_MODEL, c0, c0 + D_MODEL)
    reqs = [w_in(_V), w_in(_U), w_in(_XA), w_in(_CA), w_in(_BA), full("w_out_a", D_A),
            full("w_out_b", D_B), w_in(_GA), w_in(_GB), full("w_o", D_MODEL), full("w_ple", P_DIM)]
    for c0, c1 in zip(FF_SPLITS[:-1], FF_SPLITS[1:]):
        reqs += [("w_up", 0, D_MODEL, c0, c1), ("w_up", 0, D_MODEL, D_FF + c0, D_FF + c1),
                 ("w_down", c0, c1, 0, D_MODEL)]
    reqs.append(full("w_ple_gate", D_MODEL))
    return reqs


def _sample_chunks():
    per_class = dict.fromkeys(_CLASSES, 0)
    plan = []
    for name, r0, r1, c0, c1 in _sample_requests():
        cls, axis, size = _CHUNKING[name]
        lo, hi = ((r0, r1), (c0, c1))[axis]
        chunks = []
        for s in range(lo, hi, size):
            e = min(s + size, hi)
            rows, cols = ((s, e), (c0, c1)) if axis == 0 else ((r0, r1), (s, e))
            chunks.append((name, cls, per_class[cls], rows, cols))
            per_class[cls] += 1
        plan.append(chunks)
    return plan, per_class


def _ring_depth(n_chunks, depth):
    return min(n_chunks, depth)


class _WeightStream:
    def __init__(self, w_hbm, packed_hbm, stage, pstage, in_sems, out_sems):
        self._w, self._packed = w_hbm, packed_hbm
        self._stage, self._pstage, self._in_sems, self._out_sems = stage, pstage, in_sems, out_sems
        self._plan, per_class = _sample_chunks()
        self._flat = [c for chunks in self._plan for c in chunks]
        self._in_depth = {c: _ring_depth(n, IN_FLIGHT + 1) for c, n in per_class.items()}
        self._out_depth = {c: _ring_depth(n, OUT_SLOTS) for c, n in per_class.items()}
        self._reads = {}
        self._writes = {}
        self._next_request = 0
        self._next_chunk = 0
        for j in range(IN_FLIGHT):
            self._start_read(j)

    def _start_read(self, j):
        if j >= len(self._flat):
            return
        name, cls, k, (r0, r1), (c0, c1) = self._flat[j]
        slot = k % self._in_depth[cls]
        copy = pltpu.make_async_copy(self._w[name].at[r0:r1, c0:c1],
                                     self._stage[cls].at[slot, 0:r1 - r0, 0:c1 - c0],
                                     self._in_sems[cls].at[slot])
        copy.start()
        self._reads[j] = copy

    def _take(self):
        j = self._next_chunk
        self._next_chunk += 1
        name, cls, k, (r0, r1), (c0, c1) = self._flat[j]
        nr, nc = r1 - r0, c1 - c0
        self._reads.pop(j).wait()
        w = self._stage[cls][k % self._in_depth[cls], 0:nr, 0:nc].astype(bf16)
        self._start_read(j + IN_FLIGHT)
        pslot = k % self._out_depth[cls]
        if (cls, pslot) in self._writes:
            self._writes.pop((cls, pslot)).wait()
        self._pstage[cls][pslot, 0:nr // 2, 0:nc] = pltpu.bitcast(w, jnp.uint32)
        copy = pltpu.make_async_copy(self._pstage[cls].at[pslot, 0:nr // 2, 0:nc],
                                     self._packed[name].at[r0 // 2:r1 // 2, c0:c1],
                                     self._out_sems[cls].at[pslot])
        copy.start()
        self._writes[(cls, pslot)] = copy
        return w

    def dot(self, name, a, r0, r1, c0, c1):
        chunks = self._plan[self._next_request]
        assert _sample_requests()[self._next_request] == (name, r0, r1, c0, c1)
        self._next_request += 1
        if _CHUNKING[name][1] == 1:
            return jnp.concatenate([_dot(a, self._take()) for _ in chunks], axis=1)
        acc = None
        for _, _, _, (s, e), _ in chunks:
            part = _dot(a[:, s - r0:e - r0], self._take())
            acc = part if acc is None else acc + part
        return acc

    def finish(self):
        assert self._next_chunk == len(self._flat) and not self._reads
        for copy in self._writes.values():
            copy.wait()
        self._writes = {}


class _StreamedWeight:
    def __init__(self, stream, name, shape):
        self.stream, self.name, self.shape = stream, name, shape


def _rms(x, g):
    ms = jnp.mean(x * x, axis=-1, keepdims=True)
    return x * lax.rsqrt(ms + EPS) * g


def _dot(a, b):
    return jnp.dot(a, b, preferred_element_type=f32)


def _wdot(a, w_ref, r0=0, r1=None, c0=0, c1=None):
    if isinstance(w_ref, _StreamedWeight):
        r1 = w_ref.shape[0] if r1 is None else r1
        c1 = w_ref.shape[1] if c1 is None else c1
        return w_ref.stream.dot(w_ref.name, a, r0, r1, c0, c1)
    r1 = 2 * w_ref.shape[0] if r1 is None else r1
    c1 = w_ref.shape[1] if c1 is None else c1
    w = pltpu.bitcast(w_ref[r0 // 2:r1 // 2, c0:c1], bf16)
    return _dot(a, w)


def _conv_seq(x, w, slab, col0, first):
    t, c = x.shape
    outs = []
    for j in range(c // LANES):
        jj = col0 // LANES + j
        xj = x[:, j * LANES:(j + 1) * LANES]
        wj = w[:, j * LANES:(j + 1) * LANES]
        slab[jj, 0:SUBLANES, :] = jnp.where(first, 0.0, slab[jj, t:t + SUBLANES, :])
        slab[jj, SUBLANES:SUBLANES + t, :] = xj
        outs.append(wj[0:1] * slab[jj, SUBLANES - 2:SUBLANES - 2 + t, :]
                    + wj[1:2] * slab[jj, SUBLANES - 1:SUBLANES - 1 + t, :] + wj[2:3] * xj)
    return jnp.concatenate(outs, axis=1)


def _mixers(seq, x, prev_a, refs, out):
    (g_mix, w_in, w_conv_a, w_out_a, g_v, w_sp, b_sp, w_out_b, w_o) = refs
    t = x.shape[0]
    xn = _rms(x, g_mix[...]).astype(bf16)
    yield

    v = jax.nn.gelu(_wdot(xn, w_in, c0=_V, c1=_V + D_B))
    u = jax.nn.gelu(_wdot(xn, w_in, c0=_U, c1=_U + D_B))
    v = _rms(v, g_v[...])
    out["v"] = v
    yield

    if seq:
        vb = v.astype(bf16)
        ti = lax.broadcasted_iota(jnp.int32, (CHUNK, CHUNK), 0)
        si = lax.broadcasted_iota(jnp.int32, (CHUNK, CHUNK), 1)
        causal = si <= ti
        bias = b_sp[...]
        n_c = t // CHUNK
        cols = []
        for g in range(G_B):
            wg = jnp.where(causal, w_sp[g], jnp.zeros((), bf16))
            bg = bias[:, g:g + 1]
            vg = jnp.concatenate([vb[c * CHUNK:(c + 1) * CHUNK, g * DG_B:(g + 1) * DG_B]
                                  for c in range(n_c)], axis=1)
            sg = _dot(wg, vg)
            cols.append(jnp.concatenate([sg[:, c * DG_B:(c + 1) * DG_B] + bg for c in range(n_c)],
                                        axis=0))
        s = jnp.concatenate(cols, axis=1)
    else:
        s = w_sp[...] * v + b_sp[...]
    us = (u * s).astype(bf16)

    xa = _wdot(xn, w_in, c0=_XA, c1=_XA + D_A)
    ca = _wdot(xn, w_in, c0=_CA, c1=_CA + D_A)
    cx = ca * xa
    out["cx"] = cx
    wca = w_conv_a[...]
    if seq:
        conv_a = _conv_seq(cx, wca, prev_a[0], 0, prev_a[1])
    else:
        conv_a = wca[0:1] * prev_a[0] + wca[1:2] * prev_a[1] + wca[2:3] * cx
    ba = _wdot(xn, w_in, c0=_BA, c1=_BA + D_A)
    yield

    y_a = _wdot((ba * conv_a).astype(bf16), w_out_a)
    y_b = _wdot(us, w_out_b)
    ga = _wdot(xn, w_in, c0=_GA, c1=_GA + D_MODEL)
    gb = _wdot(xn, w_in, c0=_GB, c1=_GB + D_MODEL)
    m = jax.nn.sigmoid(ga) * y_a + jax.nn.sigmoid(gb) * y_b
    yield

    out["h"] = x + _wdot(m.astype(bf16), w_o)
    yield


def _ffn_ple(seq, h, p, prev_f, refs, out):
    (g_ffn, w_up, w_conv_ffn, w_down, g_ple, w_ple_gate, w_ple, g_final) = refs
    pe = _wdot(p.astype(bf16), w_ple)
    xn2 = _rms(h, g_ffn[...]).astype(bf16)
    yield

    acc = None
    out["up"] = []
    for c0, c1 in zip(FF_SPLITS[:-1], FF_SPLITS[1:]):
        n = c1 - c0
        halves = []
        for off in (c0, D_FF + c0):
            up = _wdot(xn2, w_up, c0=off, c1=off + n)
            wcf = w_conv_ffn[:, off:off + n]
            if seq:
                conv = _conv_seq(up, wcf, prev_f[0], off, prev_f[1])
            else:
                conv = (wcf[0:1] * prev_f[0][:, off:off + n]
                        + wcf[1:2] * prev_f[1][:, off:off + n] + wcf[2:3] * up)
            out["up"].append((off, up))
            halves.append(conv)
        hid = (jax.nn.gelu(halves[0]) * halves[1]).astype(bf16)
        yield
        part = _wdot(hid, w_down, r0=c0, r1=c1)
        acc = part if acc is None else acc + part
        yield
    h = h + acc

    gate = jax.nn.sigmoid(_wdot(_rms(h, g_ple[...]).astype(bf16), w_ple_gate))
    yield
    h = h + gate * pe
    out["y"] = _rms(h, g_final[...])
    yield


def _run(schedule, **gens):
    for name in schedule:
        next(gens[name])
    for g in gens.values():
        for _ in g:
            pass


PIPELINE_ORDER = "FMFMFFMFMFMF"

WEIGHT_ORDER = ("g_mix", "w_in", "w_conv_a", "w_out_a", "g_v", "w_sp", "b_sp", "w_out_b", "w_o",
                "g_ffn", "w_up", "w_conv_ffn", "w_down", "g_ple", "w_ple_gate", "w_ple", "g_final")
N_MIX_REFS = 9
N_FFN_REFS = len(WEIGHT_ORDER) - N_MIX_REFS
MATMUL_WEIGHTS = tuple(_CHUNKING)


def _prompt_kernel(tiles_per_seq, x_ref, p_ref, *rest):
    mix_refs = rest[:N_MIX_REFS]
    ffn_refs = rest[N_MIX_REFS:N_MIX_REFS + N_FFN_REFS]
    y_ref, ca_ref, cv_ref, cf_ref, slab_a, slab_f, h_scr = rest[N_MIX_REFS + N_FFN_REFS:]
    i = pl.program_id(0)
    last = pl.num_programs(0) - 1

    @pl.when(i == 0)
    def _():
        slab_a[:, TM:TM + SUBLANES, :] = jnp.zeros((slab_a.shape[0], SUBLANES, LANES), f32)
        slab_f[:, TM:TM + SUBLANES, :] = jnp.zeros((slab_f.shape[0], SUBLANES, LANES), f32)
        h_scr[...] = jnp.zeros_like(h_scr)

    def ffn_of_previous_tile(out):
        first = (jnp.maximum(i - 1, 0) % tiles_per_seq) == 0
        return _ffn_ple(True, h_scr[...], p_ref[0], (slab_f, first), ffn_refs, out)

    def store_ffn(out):
        y_ref[0] = out["y"]
        for off, up in out["up"]:
            cf_ref[0, :, off:off + up.shape[1]] = up[TM - (CONV_W - 1):TM]

    @pl.when(i < last)
    def _():
        first = (i % tiles_per_seq) == 0
        mo, fo = {}, {}
        _run(PIPELINE_ORDER, M=_mixers(True, x_ref[0], (slab_a, first), mix_refs, mo),
             F=ffn_of_previous_tile(fo))
        store_ffn(fo)
        h_scr[...] = mo["h"]
        cx, v = mo["cx"], mo["v"]
        ca_ref[0] = cx[TM - (CONV_W - 1):TM]
        cv_ref[0] = v[TM - CHUNK:TM]

    @pl.when(i == last)
    def _():
        fo = {}
        _run("", F=ffn_of_previous_tile(fo))
        store_ffn(fo)


def _sample_kernel(x_ref, p_ref, sa_ref, sf_ref, *rest):
    n_w, n_m, n_c = N_MIX_REFS + N_FFN_REFS, len(MATMUL_WEIGHTS), len(_CLASSES)
    weights = list(rest[:n_w])
    y_ref, ca_ref, cv_ref, cf_ref = rest[n_w:n_w + 4]
    packed = dict(zip(MATMUL_WEIGHTS, rest[n_w + 4:n_w + 4 + n_m]))
    scratch = rest[n_w + 4 + n_m:]
    stage, pstage, in_sems, out_sems = (dict(zip(_CLASSES, scratch[k * n_c:(k + 1) * n_c]))
                                        for k in range(4))
    w_hbm = {name: weights[WEIGHT_ORDER.index(name)] for name in MATMUL_WEIGHTS}
    stream = _WeightStream(w_hbm, packed, stage, pstage, in_sems, out_sems)
    for name in MATMUL_WEIGHTS:
        weights[WEIGHT_ORDER.index(name)] = _StreamedWeight(stream, name, w_hbm[name].shape)
    mix_refs = weights[:N_MIX_REFS]
    ffn_refs = weights[N_MIX_REFS:]
    sa0, sa1 = sa_ref[:, 0, :], sa_ref[:, 1, :]
    sf0, sf1 = sf_ref[:, 0, :], sf_ref[:, 1, :]
    mo, fo = {}, {}
    _run("", M=_mixers(False, x_ref[:, 0, :], (sa0, sa1), mix_refs, mo))
    _run("", F=_ffn_ple(False, mo["h"], p_ref[:, 0, :], (sf0, sf1), ffn_refs, fo))
    y_ref[:, 0, :] = fo["y"]
    ca_ref[:, 0, :] = sa1
    ca_ref[:, 1, :] = mo["cx"]
    cv_ref[:, 0, :] = mo["v"]
    cf_ref[:, 0, :] = sf1
    for off, up in fo["up"]:
        cf_ref[:, 1, off:off + up.shape[1]] = up
    stream.finish()


def _full(shape):
    return pl.BlockSpec(shape, lambda *_: (0,) * len(shape))


def kernel(x_prompt, x_sample, p_prompt, p_sample, state_conv_a, state_conv_ffn, g_mix, w_in, w_conv_a, w_out_a, g_v, w_spatial, b_spatial, w_out_b, w_o, g_ffn, w_up, w_conv_ffn, w_down, g_ple, w_ple_gate, w_ple, g_final):
    depth = g_mix.shape[0]
    assert depth == 1
    batch, seq_len, _ = x_prompt.shape
    dec_batch, dec_seq, _ = x_sample.shape
    assert dec_seq == 1 and seq_len % TM == 0 and TM % CHUNK == 0

    row = lambda a: a.reshape(1, -1)
    small = dict(g_mix=row(g_mix[0]), w_conv_a=w_conv_a[0], g_v=row(g_v[0]), g_ffn=row(g_ffn[0]),
                 w_conv_ffn=w_conv_ffn[0], g_ple=row(g_ple[0]), g_final=row(g_final))
    mats = dict(w_in=w_in[0], w_out_a=w_out_a[0], w_out_b=w_out_b[0], w_o=w_o[0], w_up=w_up[0],
                w_down=w_down[0], w_ple_gate=w_ple_gate[0], w_ple=w_ple[0])

    sw = dict(small, **mats,
              w_sp=row(jnp.repeat(w_spatial[0, :, 0, 0], DG_B)),
              b_sp=row(jnp.repeat(b_spatial[0, :, 0], DG_B)))
    sw_args = [sw[k] for k in WEIGHT_ORDER]
    s_in = [x_sample, p_sample[0], state_conv_a[0], state_conv_ffn[0]]
    s_out = [(dec_batch, 1, D_MODEL), (dec_batch, CONV_W - 1, D_A), (dec_batch, 1, D_B),
             (dec_batch, CONV_W - 1, 2 * D_FF)]
    packed_shapes = [(mats[k].shape[0] // 2, mats[k].shape[1]) for k in MATMUL_WEIGHTS]
    _, per_class = _sample_chunks()
    in_depth = {c: _ring_depth(per_class[c], IN_FLIGHT + 1) for c in _CLASSES}
    out_depth = {c: _ring_depth(per_class[c], OUT_SLOTS) for c in _CLASSES}
    hbm = pl.BlockSpec(memory_space=pl.ANY)
    outs = pl.pallas_call(
        _sample_kernel,
        in_specs=[_full(a.shape) for a in s_in]
                 + [hbm if k in MATMUL_WEIGHTS else _full(sw[k].shape) for k in WEIGHT_ORDER],
        out_specs=[_full(s) for s in s_out] + [hbm] * len(MATMUL_WEIGHTS),
        out_shape=[jax.ShapeDtypeStruct(s, f32) for s in s_out]
                  + [jax.ShapeDtypeStruct(s, jnp.uint32) for s in packed_shapes],
        scratch_shapes=[pltpu.VMEM((in_depth[c],) + _CLASS_SHAPE[c], f32) for c in _CLASSES]
                       + [pltpu.VMEM((out_depth[c], _CLASS_SHAPE[c][0] // 2, _CLASS_SHAPE[c][1]),
                                     jnp.uint32) for c in _CLASSES]
                       + [pltpu.SemaphoreType.DMA((in_depth[c],)) for c in _CLASSES]
                       + [pltpu.SemaphoreType.DMA((out_depth[c],)) for c in _CLASSES],
        compiler_params=pltpu.CompilerParams(vmem_limit_bytes=VMEM_LIMIT_BYTES),
        name="layer_sample",
    )(*s_in, *sw_args)
    y_s, ca_s, cv_s, cf_s = outs[:4]
    packed = dict(zip(MATMUL_WEIGHTS, outs[4:]))

    pw = dict(small, **packed, w_sp=w_spatial[0].astype(bf16), b_sp=b_spatial[0].T)
    pw_args = [pw[k] for k in WEIGHT_ORDER]
    n_l = seq_len // TM
    n_tiles = batch * n_l

    def mix_tile(i):
        return jnp.minimum(i, n_tiles - 1)

    def ffn_tile(i):
        return jnp.maximum(i - 1, 0)

    y_p, ca_p, cv_p, cf_p = pl.pallas_call(
        functools.partial(_prompt_kernel, n_l),
        grid=(n_tiles + 1,),
        in_specs=[pl.BlockSpec((1, TM, D_MODEL), lambda i: (mix_tile(i) // n_l, mix_tile(i) % n_l, 0)),
                  pl.BlockSpec((1, TM, P_DIM), lambda i: (ffn_tile(i) // n_l, ffn_tile(i) % n_l, 0))]
                 + [_full(a.shape) for a in pw_args],
        out_specs=[pl.BlockSpec((1, TM, D_MODEL), lambda i: (ffn_tile(i) // n_l, ffn_tile(i) % n_l, 0)),
                   pl.BlockSpec((1, CONV_W - 1, D_A), lambda i: (mix_tile(i) // n_l, 0, 0)),
                   pl.BlockSpec((1, CHUNK, D_B), lambda i: (mix_tile(i) // n_l, 0, 0)),
                   pl.BlockSpec((1, CONV_W - 1, 2 * D_FF), lambda i: (ffn_tile(i) // n_l, 0, 0))],
        out_shape=[jax.ShapeDtypeStruct((batch, seq_len, D_MODEL), f32),
                   jax.ShapeDtypeStruct((batch, CONV_W - 1, D_A), f32),
                   jax.ShapeDtypeStruct((batch, CHUNK, D_B), f32),
                   jax.ShapeDtypeStruct((batch, CONV_W - 1, 2 * D_FF), f32)],
        scratch_shapes=[pltpu.VMEM((D_A // LANES, TM + SUBLANES, LANES), f32),
                        pltpu.VMEM((2 * D_FF // LANES, TM + SUBLANES, LANES), f32),
                        pltpu.VMEM((TM, D_MODEL), f32)],
        compiler_params=pltpu.CompilerParams(
            dimension_semantics=("arbitrary",), vmem_limit_bytes=VMEM_LIMIT_BYTES),
        name="layer_prompt",
    )(x_prompt, p_prompt[0], *pw_args)

    return (y_p, y_s, ca_p[None], ca_s[None], cv_p[None], cv_s[None], cf_p[None], cf_s[None])
```

```python
import functools

import jax
import jax.numpy as jnp
from jax import lax
from jax.experimental import pallas as pl
from jax.experimental.pallas import tpu as pltpu

D_MODEL = 1024
D_A = 1024
D_B = 1024
G_B = 8
DG_B = D_B // G_B
CHUNK = 128
CONV_W = 3
D_FF = 2816
P_DIM = 256
EPS = 1e-6

SUBLANES = 8
LANES = 128
TM = 256
MXU_N = 256
FF_SPLITS = (0, 3 * MXU_N, 6 * MXU_N, 9 * MXU_N, D_FF)
VMEM_LIMIT_BYTES = 62 * 1024 * 1024

_XA, _BA, _CA, _U, _V, _GA, _GB = (0, D_A, 2 * D_A, 3 * D_A, 3 * D_A + D_B,
                                   3 * D_A + 2 * D_B, 3 * D_A + 2 * D_B + D_MODEL)

bf16 = jnp.bfloat16
f32 = jnp.float32


IN_FLIGHT = 4
OUT_SLOTS = 2
_CHUNKING = {"w_in": ("wide", 1, 512), "w_up": ("wide", 1, 512),
             "w_out_a": ("tall", 0, 512), "w_out_b": ("tall", 0, 512), "w_o": ("tall", 0, 512),
             "w_ple_gate": ("tall", 0, 512), "w_down": ("tall", 0, 512), "w_ple": ("ple", 0, 256)}
_CLASS_SHAPE = {"wide": (D_MODEL, 512), "tall": (512, D_MODEL), "ple": (P_DIM, D_MODEL)}
_CLASSES = tuple(_CLASS_SHAPE)


def _sample_requests():
    full = lambda name, k: (name, 0, k, 0, D_MODEL)
    w_in = lambda c0: ("w_in", 0, D_MODEL, c0, c0 + D_MODEL)
    reqs = [w_in(_V), w_in(_U), w_in(_XA), w_in(_CA), w_in(_BA), w_in(_GA), w_in(_GB),
            full("w_out_a", D_A), full("w_out_b", D_B), full("w_o", D_MODEL), full("w_ple", P_DIM)]
    splits = list(zip(FF_SPLITS[:-1], FF_SPLITS[1:]))
    for k, (c0, c1) in enumerate(splits):
        reqs += [("w_up", 0, D_MODEL, c0, c1), ("w_up", 0, D_MODEL, D_FF + c0, D_FF + c1)]
        if k:
            reqs.append(("w_down",) + splits[k - 1] + (0, D_MODEL))
    reqs += [("w_down",) + splits[-1] + (0, D_MODEL), full("w_ple_gate", D_MODEL)]
    return reqs


def _sample_chunks():
    per_class = dict.fromkeys(_CLASSES, 0)
    plan = []
    for name, r0, r1, c0, c1 in _sample_requests():
        cls, axis, size = _CHUNKING[name]
        lo, hi = ((r0, r1), (c0, c1))[axis]
        chunks = []
        for s in range(lo, hi, size):
            e = min(s + size, hi)
            rows, cols = ((s, e), (c0, c1)) if axis == 0 else ((r0, r1), (s, e))
            chunks.append((name, cls, per_class[cls], rows, cols))
            per_class[cls] += 1
        plan.append(chunks)
    return plan, per_class


def _ring_depth(n_chunks, depth):
    return min(n_chunks, depth)


class _WeightStream:
    def __init__(self, w_hbm, packed_hbm, stage, pstage, in_sems, out_sems):
        self._w, self._packed = w_hbm, packed_hbm
        self._stage, self._pstage, self._in_sems, self._out_sems = stage, pstage, in_sems, out_sems
        self._plan, per_class = _sample_chunks()
        self._flat = [c for chunks in self._plan for c in chunks]
        self._in_depth = {c: _ring_depth(n, IN_FLIGHT + 1) for c, n in per_class.items()}
        self._out_depth = {c: _ring_depth(n, OUT_SLOTS) for c, n in per_class.items()}
        self._reads = {}
        self._writes = {}
        self._next_request = 0
        self._next_chunk = 0
        for j in range(IN_FLIGHT):
            self._start_read(j)

    def _start_read(self, j):
        if j >= len(self._flat):
            return
        name, cls, k, (r0, r1), (c0, c1) = self._flat[j]
        slot = k % self._in_depth[cls]
        copy = pltpu.make_async_copy(self._w[name].at[r0:r1, c0:c1],
                                     self._stage[cls].at[slot, 0:r1 - r0, 0:c1 - c0],
                                     self._in_sems[cls].at[slot])
        copy.start()
        self._reads[j] = copy

    def _take(self):
        j = self._next_chunk
        self._next_chunk += 1
        name, cls, k, (r0, r1), (c0, c1) = self._flat[j]
        nr, nc = r1 - r0, c1 - c0
        self._reads.pop(j).wait()
        w = self._stage[cls][k % self._in_depth[cls], 0:nr, 0:nc].astype(bf16)
        self._start_read(j + IN_FLIGHT)
        pslot = k % self._out_depth[cls]
        if (cls, pslot) in self._writes:
            self._writes.pop((cls, pslot)).wait()
        self._pstage[cls][pslot, 0:nr // 2, 0:nc] = pltpu.bitcast(w, jnp.uint32)
        copy = pltpu.make_async_copy(self._pstage[cls].at[pslot, 0:nr // 2, 0:nc],
                                     self._packed[name].at[r0 // 2:r1 // 2, c0:c1],
                                     self._out_sems[cls].at[pslot])
        copy.start()
        self._writes[(cls, pslot)] = copy
        return w

    def dot(self, name, a, r0, r1, c0, c1):
        chunks = self._plan[self._next_request]
        assert _sample_requests()[self._next_request] == (name, r0, r1, c0, c1)
        self._next_request += 1
        if _CHUNKING[name][1] == 1:
            return jnp.concatenate([_dot(a, self._take()) for _ in chunks], axis=1)
        acc = None
        for _, _, _, (s, e), _ in chunks:
            part = _dot(a[:, s - r0:e - r0], self._take())
            acc = part if acc is None else acc + part
        return acc

    def finish(self):
        assert self._next_chunk == len(self._flat) and not self._reads
        for copy in self._writes.values():
            copy.wait()
        self._writes = {}


class _StreamedWeight:
    def __init__(self, stream, name, shape):
        self.stream, self.name, self.shape = stream, name, shape


def _rms(x, g):
    ms = jnp.mean(x * x, axis=-1, keepdims=True)
    return x * lax.rsqrt(ms + EPS) * g


def _dot(a, b):
    return jnp.dot(a, b, preferred_element_type=f32)


def _wdot(a, w_ref, r0=0, r1=None, c0=0, c1=None):
    if isinstance(w_ref, _StreamedWeight):
        r1 = w_ref.shape[0] if r1 is None else r1
        c1 = w_ref.shape[1] if c1 is None else c1
        return w_ref.stream.dot(w_ref.name, a, r0, r1, c0, c1)
    r1 = 2 * w_ref.shape[0] if r1 is None else r1
    c1 = w_ref.shape[1] if c1 is None else c1
    w = pltpu.bitcast(w_ref[r0 // 2:r1 // 2, c0:c1], bf16)
    return _dot(a, w)


def _conv_seq(x, w, slab, col0, first):
    t, c = x.shape
    outs = []
    for j in range(c // LANES):
        jj = col0 // LANES + j
        xj = x[:, j * LANES:(j + 1) * LANES]
        wj = w[:, j * LANES:(j + 1) * LANES]
        slab[jj, 0:SUBLANES, :] = jnp.where(first, 0.0, slab[jj, t:t + SUBLANES, :])
        slab[jj, SUBLANES:SUBLANES + t, :] = xj
        outs.append(wj[0:1] * slab[jj, SUBLANES - 2:SUBLANES - 2 + t, :]
                    + wj[1:2] * slab[jj, SUBLANES - 1:SUBLANES - 1 + t, :] + wj[2:3] * xj)
    return jnp.concatenate(outs, axis=1)


def _mixers(seq, x, prev_a, refs, out):
    (g_mix, w_in, w_conv_a, w_out_a, g_v, w_sp, b_sp, w_out_b, w_o) = refs
    t = x.shape[0]
    xn = _rms(x, g_mix[...]).astype(bf16)
    yield

    v = jax.nn.gelu(_wdot(xn, w_in, c0=_V, c1=_V + D_B))
    u = jax.nn.gelu(_wdot(xn, w_in, c0=_U, c1=_U + D_B))
    v = _rms(v, g_v[...])
    out["v"] = v
    yield

    if seq:
        vb = v.astype(bf16)
        ti = lax.broadcasted_iota(jnp.int32, (CHUNK, CHUNK), 0)
        si = lax.broadcasted_iota(jnp.int32, (CHUNK, CHUNK), 1)
        causal = si <= ti
        bias = b_sp[...]
        n_c = t // CHUNK
        cols = []
        for g in range(G_B):
            wg = jnp.where(causal, w_sp[g], jnp.zeros((), bf16))
            bg = bias[:, g:g + 1]
            vg = jnp.concatenate([vb[c * CHUNK:(c + 1) * CHUNK, g * DG_B:(g + 1) * DG_B]
                                  for c in range(n_c)], axis=1)
            sg = _dot(wg, vg)
            cols.append(jnp.concatenate([sg[:, c * DG_B:(c + 1) * DG_B] + bg for c in range(n_c)],
                                        axis=0))
        s = jnp.concatenate(cols, axis=1)
    else:
        s = w_sp[...] * v + b_sp[...]
    us = (u * s).astype(bf16)

    xa = _wdot(xn, w_in, c0=_XA, c1=_XA + D_A)
    yield

    ca = _wdot(xn, w_in, c0=_CA, c1=_CA + D_A)
    cx = ca * xa
    out["cx"] = cx
    wca = w_conv_a[...]
    if seq:
        conv_a = _conv_seq(cx, wca, prev_a[0], 0, prev_a[1])
    else:
        conv_a = wca[0:1] * prev_a[0] + wca[1:2] * prev_a[1] + wca[2:3] * cx
    ba = _wdot(xn, w_in, c0=_BA, c1=_BA + D_A)
    yield

    ga = _wdot(xn, w_in, c0=_GA, c1=_GA + D_MODEL)
    gb = _wdot(xn, w_in, c0=_GB, c1=_GB + D_MODEL)
    yield

    y_a = _wdot((ba * conv_a).astype(bf16), w_out_a)
    y_b = _wdot(us, w_out_b)
    m = jax.nn.sigmoid(ga) * y_a + jax.nn.sigmoid(gb) * y_b
    yield

    out["h"] = x + _wdot(m.astype(bf16), w_o)
    yield


def _ffn_ple(seq, h, p, prev_f, refs, out):
    (g_ffn, w_up, w_conv_ffn, w_down, g_ple, w_ple_gate, w_ple, g_final) = refs
    pe = _wdot(p.astype(bf16), w_ple)
    xn2 = _rms(h, g_ffn[...]).astype(bf16)
    yield

    acc = None
    hid_of_previous_chunk = None
    out["up"] = []

    def down(acc, hid, c0, c1):
        part = _wdot(hid, w_down, r0=c0, r1=c1)
        return part if acc is None else acc + part

    for c0, c1 in zip(FF_SPLITS[:-1], FF_SPLITS[1:]):
        n = c1 - c0
        halves = []
        for off in (c0, D_FF + c0):
            up = _wdot(xn2, w_up, c0=off, c1=off + n)
            wcf = w_conv_ffn[:, off:off + n]
            if seq:
                conv = _conv_seq(up, wcf, prev_f[0], off, prev_f[1])
            else:
                conv = (wcf[0:1] * prev_f[0][:, off:off + n]
                        + wcf[1:2] * prev_f[1][:, off:off + n] + wcf[2:3] * up)
            out["up"].append((off, up))
            halves.append(conv)
        hid = (jax.nn.gelu(halves[0]) * halves[1]).astype(bf16)
        yield
        if hid_of_previous_chunk is not None:
            acc = down(acc, *hid_of_previous_chunk)
            yield
        hid_of_previous_chunk = (hid, c0, c1)
    acc = down(acc, *hid_of_previous_chunk)
    yield
    h = h + acc

    gate = jax.nn.sigmoid(_wdot(_rms(h, g_ple[...]).astype(bf16), w_ple_gate))
    yield
    h = h + gate * pe
    out["y"] = _rms(h, g_final[...])
    yield


def _run(schedule, **gens):
    for name in schedule:
        next(gens[name])
    for g in gens.values():
        for _ in g:
            pass


PIPELINE_ORDER = "FMFMFFMFFMFFMFMFMF"

WEIGHT_ORDER = ("g_mix", "w_in", "w_conv_a", "w_out_a", "g_v", "w_sp", "b_sp", "w_out_b", "w_o",
                "g_ffn", "w_up", "w_conv_ffn", "w_down", "g_ple", "w_ple_gate", "w_ple", "g_final")
N_MIX_REFS = 9
N_FFN_REFS = len(WEIGHT_ORDER) - N_MIX_REFS
MATMUL_WEIGHTS = tuple(_CHUNKING)


def _prompt_kernel(tiles_per_seq, x_ref, p_ref, *rest):
    mix_refs = rest[:N_MIX_REFS]
    ffn_refs = rest[N_MIX_REFS:N_MIX_REFS + N_FFN_REFS]
    y_ref, ca_ref, cv_ref, cf_ref, slab_a, slab_f, h_scr = rest[N_MIX_REFS + N_FFN_REFS:]
    i = pl.program_id(0)
    last = pl.num_programs(0) - 1

    @pl.when(i == 0)
    def _():
        slab_a[:, TM:TM + SUBLANES, :] = jnp.zeros((slab_a.shape[0], SUBLANES, LANES), f32)
        slab_f[:, TM:TM + SUBLANES, :] = jnp.zeros((slab_f.shape[0], SUBLANES, LANES), f32)
        h_scr[...] = jnp.zeros_like(h_scr)

    def ffn_of_previous_tile(out):
        first = (jnp.maximum(i - 1, 0) % tiles_per_seq) == 0
        return _ffn_ple(True, h_scr[...], p_ref[0], (slab_f, first), ffn_refs, out)

    def store_ffn(out):
        y_ref[0] = out["y"]
        for off, up in out["up"]:
            cf_ref[0, :, off:off + up.shape[1]] = up[TM - (CONV_W - 1):TM]

    @pl.when(i < last)
    def _():
        first = (i % tiles_per_seq) == 0
        mo, fo = {}, {}
        _run(PIPELINE_ORDER, M=_mixers(True, x_ref[0], (slab_a, first), mix_refs, mo),
             F=ffn_of_previous_tile(fo))
        store_ffn(fo)
        h_scr[...] = mo["h"]
        cx, v = mo["cx"], mo["v"]
        ca_ref[0] = cx[TM - (CONV_W - 1):TM]
        cv_ref[0] = v[TM - CHUNK:TM]

    @pl.when(i == last)
    def _():
        fo = {}
        _run("", F=ffn_of_previous_tile(fo))
        store_ffn(fo)


def _sample_kernel(x_ref, p_ref, sa_ref, sf_ref, *rest):
    n_w, n_m, n_c = N_MIX_REFS + N_FFN_REFS, len(MATMUL_WEIGHTS), len(_CLASSES)
    weights = list(rest[:n_w])
    y_ref, ca_ref, cv_ref, cf_ref = rest[n_w:n_w + 4]
    packed = dict(zip(MATMUL_WEIGHTS, rest[n_w + 4:n_w + 4 + n_m]))
    scratch = rest[n_w + 4 + n_m:]
    stage, pstage, in_sems, out_sems = (dict(zip(_CLASSES, scratch[k * n_c:(k + 1) * n_c]))
                                        for k in range(4))
    w_hbm = {name: weights[WEIGHT_ORDER.index(name)] for name in MATMUL_WEIGHTS}
    stream = _WeightStream(w_hbm, packed, stage, pstage, in_sems, out_sems)
    for name in MATMUL_WEIGHTS:
        weights[WEIGHT_ORDER.index(name)] = _StreamedWeight(stream, name, w_hbm[name].shape)
    mix_refs = weights[:N_MIX_REFS]
    ffn_refs = weights[N_MIX_REFS:]
    sa0, sa1 = sa_ref[:, 0, :], sa_ref[:, 1, :]
    sf0, sf1 = sf_ref[:, 0, :], sf_ref[:, 1, :]
    mo, fo = {}, {}
    _run("", M=_mixers(False, x_ref[:, 0, :], (sa0, sa1), mix_refs, mo))
    _run("", F=_ffn_ple(False, mo["h"], p_ref[:, 0, :], (sf0, sf1), ffn_refs, fo))
    y_ref[:, 0, :] = fo["y"]
    ca_ref[:, 0, :] = sa1
    ca_ref[:, 1, :] = mo["cx"]
    cv_ref[:, 0, :] = mo["v"]
    cf_ref[:, 0, :] = sf1
    for off, up in fo["up"]:
        cf_ref[:, 1, off:off + up.shape[1]] = up
    stream.finish()


def _full(shape):
    return pl.BlockSpec(shape, lambda *_: (0,) * len(shape))


def kernel(x_prompt, x_sample, p_prompt, p_sample, state_conv_a, state_conv_ffn, g_mix, w_in, w_conv_a, w_out_a, g_v, w_spatial, b_spatial, w_out_b, w_o, g_ffn, w_up, w_conv_ffn, w_down, g_ple, w_ple_gate, w_ple, g_final):
    depth = g_mix.shape[0]
    assert depth == 1
    batch, seq_len, _ = x_prompt.shape
    dec_batch, dec_seq, _ = x_sample.shape
    assert dec_seq == 1 and seq_len % TM == 0 and TM % CHUNK == 0

    row = lambda a: a.reshape(1, -1)
    small = dict(g_mix=row(g_mix[0]), w_conv_a=w_conv_a[0], g_v=row(g_v[0]), g_ffn=row(g_ffn[0]),
                 w_conv_ffn=w_conv_ffn[0], g_ple=row(g_ple[0]), g_final=row(g_final))
    mats = dict(w_in=w_in[0], w_out_a=w_out_a[0], w_out_b=w_out_b[0], w_o=w_o[0], w_up=w_up[0],
                w_down=w_down[0], w_ple_gate=w_ple_gate[0], w_ple=w_ple[0])

    sw = dict(small, **mats,
              w_sp=row(jnp.repeat(w_spatial[0, :, 0, 0], DG_B)),
              b_sp=row(jnp.repeat(b_spatial[0, :, 0], DG_B)))
    sw_args = [sw[k] for k in WEIGHT_ORDER]
    s_in = [x_sample, p_sample[0], state_conv_a[0], state_conv_ffn[0]]
    s_out = [(dec_batch, 1, D_MODEL), (dec_batch, CONV_W - 1, D_A), (dec_batch, 1, D_B),
             (dec_batch, CONV_W - 1, 2 * D_FF)]
    packed_shapes = [(mats[k].shape[0] // 2, mats[k].shape[1]) for k in MATMUL_WEIGHTS]
    _, per_class = _sample_chunks()
    in_depth = {c: _ring_depth(per_class[c], IN_FLIGHT + 1) for c in _CLASSES}
    out_depth = {c: _ring_depth(per_class[c], OUT_SLOTS) for c in _CLASSES}
    hbm = pl.BlockSpec(memory_space=pl.ANY)
    outs = pl.pallas_call(
        _sample_kernel,
        in_specs=[_full(a.shape) for a in s_in]
                 + [hbm if k in MATMUL_WEIGHTS else _full(sw[k].shape) for k in WEIGHT_ORDER],
        out_specs=[_full(s) for s in s_out] + [hbm] * len(MATMUL_WEIGHTS),
        out_shape=[jax.ShapeDtypeStruct(s, f32) for s in s_out]
                  + [jax.ShapeDtypeStruct(s, jnp.uint32) for s in packed_shapes],
        scratch_shapes=[pltpu.VMEM((in_depth[c],) + _CLASS_SHAPE[c], f32) for c in _CLASSES]
                       + [pltpu.VMEM((out_depth[c], _CLASS_SHAPE[c][0] // 2, _CLASS_SHAPE[c][1]),
                                     jnp.uint32) for c in _CLASSES]
                       + [pltpu.SemaphoreType.DMA((in_depth[c],)) for c in _CLASSES]
                       + [pltpu.SemaphoreType.DMA((out_depth[c],)) for c in _CLASSES],
        compiler_params=pltpu.CompilerParams(vmem_limit_bytes=VMEM_LIMIT_BYTES),
        name="layer_sample",
    )(*s_in, *sw_args)
    y_s, ca_s, cv_s, cf_s = outs[:4]
    packed = dict(zip(MATMUL_WEIGHTS, outs[4:]))

    pw = dict(small, **packed, w_sp=w_spatial[0].astype(bf16), b_sp=b_spatial[0].T)
    pw_args = [pw[k] for k in WEIGHT_ORDER]
    n_l = seq_len // TM
    n_tiles = batch * n_l

    def mix_tile(i):
        return jnp.minimum(i, n_tiles - 1)

    def ffn_tile(i):
        return jnp.maximum(i - 1, 0)

    y_p, ca_p, cv_p, cf_p = pl.pallas_call(
        functools.partial(_prompt_kernel, n_l),
        grid=(n_tiles + 1,),
        in_specs=[pl.BlockSpec((1, TM, D_MODEL), lambda i: (mix_tile(i) // n_l, mix_tile(i) % n_l, 0)),
                  pl.BlockSpec((1, TM, P_DIM), lambda i: (ffn_tile(i) // n_l, ffn_tile(i) % n_l, 0))]
                 + [_full(a.shape) for a in pw_args],
        out_specs=[pl.BlockSpec((1, TM, D_MODEL), lambda i: (ffn_tile(i) // n_l, ffn_tile(i) % n_l, 0)),
                   pl.BlockSpec((1, CONV_W - 1, D_A), lambda i: (mix_tile(i) // n_l, 0, 0)),
                   pl.BlockSpec((1, CHUNK, D_B), lambda i: (mix_tile(i) // n_l, 0, 0)),
                   pl.BlockSpec((1, CONV_W - 1, 2 * D_FF), lambda i: (ffn_tile(i) // n_l, 0, 0))],
        out_shape=[jax.ShapeDtypeStruct((batch, seq_len, D_MODEL), f32),
                   jax.ShapeDtypeStruct((batch, CONV_W - 1, D_A), f32),
                   jax.ShapeDtypeStruct((batch, CHUNK, D_B), f32),
                   jax.ShapeDtypeStruct((batch, CONV_W - 1, 2 * D_FF), f32)],
        scratch_shapes=[pltpu.VMEM((D_A // LANES, TM + SUBLANES, LANES), f32),
                        pltpu.VMEM((2 * D_FF // LANES, TM + SUBLANES, LANES), f32),
                        pltpu.VMEM((TM, D_MODEL), f32)],
        compiler_params=pltpu.CompilerParams(
            dimension_semantics=("arbitrary",), vmem_limit_bytes=VMEM_LIMIT_BYTES),
        name="layer_prompt",
    )(x_prompt, p_prompt[0], *pw_args)

    return (y_p, y_s, ca_p[None], ca_s[None], cv_p[None], cv_s[None], cf_p[None], cf_s[None])
```
